```python
import math
import jax, jax.numpy as jnp
from jax import lax
import numpy as np

D_MODEL = 4096
BATCH = 8
SEQ = 2048
DEPTH = 2

HEAD_DIM = 128
N_HEADS_DIFF = D_MODEL // (4 * HEAD_DIM)
N_HEADS_DIL = D_MODEL // (2 * HEAD_DIM)
N_HEADS_NA = D_MODEL // HEAD_DIM
DIL_BRANCHES = ((128, 1), (512, 4), (2048, 16))
QBLK = 128
GRID_W = 64
NA_KH = 8
NA_KW = 16
T5_BUCKETS = 32
T5_MAX_DIST = 1024
N_EXPERTS = 64
N_GROUPS = 8
TOPK_GROUPS = 4
TOP_K = 8
EXPERT_DIM = 384
SHARED_DIM = 384
ROUTED_SCALE = 2.5
ALPHA = (2 * DEPTH) ** 0.25
BETA = (8 * DEPTH) ** -0.25
LN_EPS = 1e-5
N_EVEN = (DEPTH + 1) // 2
N_ODD = DEPTH // 2
D_IN_EVEN = 3 * (N_HEADS_DIFF * 2 * HEAD_DIM) + 3 * (N_HEADS_DIL * HEAD_DIM)
D_IN_ODD = 3 * N_HEADS_NA * HEAD_DIM

kernel_name = 'hybrid_diff_dilated_natten_moe_encoder'


def layer_norm(x, g, b):
    x32 = x.astype(jnp.float32)
    mu = jnp.mean(x32, axis=-1, keepdims=True)
    var = jnp.mean(jnp.square(x32 - mu), axis=-1, keepdims=True)
    y = (x32 - mu) * lax.rsqrt(var + LN_EPS) * g.astype(jnp.float32) + b.astype(jnp.float32)
    return y.astype(x.dtype)


def t5_bias(rel, table):
    half = T5_BUCKETS // 2
    exact = half // 2
    n = jnp.abs(rel)
    log_ratio = jnp.log(jnp.maximum(n, 1).astype(jnp.float32) / exact) / math.log(T5_MAX_DIST / exact)
    large = jnp.minimum(exact + (log_ratio * (half - exact)).astype(jnp.int32), half - 1)
    bucket = jnp.where(rel > 0, half, 0) + jnp.where(n < exact, n, large)
    return jnp.moveaxis(table[bucket].astype(jnp.float32), -1, 0)


def diff_attention(q, k, v, table, lam, gain, lam_init):
    B, S, H, _, dh = q.shape
    nq = S // QBLK
    qb = jnp.moveaxis(q.reshape(B, nq, QBLK, H, 2, dh), 1, 0)
    kpos = jnp.arange(S)

    def block(args):
        i, qi = args
        qpos = i * QBLK + jnp.arange(QBLK)
        bias = t5_bias(kpos[None, :] - qpos[:, None], table)
        s = jnp.einsum('bqhmc,bkhmc->bhmqk', qi, k).astype(jnp.float32) * dh ** -0.5
        p = jax.nn.softmax(s + bias[None, :, None], axis=-1)
        a = p[:, :, 0] - lam * p[:, :, 1]
        return jnp.einsum('bhqk,bkhc->bqhc', a.astype(v.dtype), v)

    o = lax.map(block, (jnp.arange(nq), qb))
    o = jnp.moveaxis(o, 0, 1).reshape(B, S, H, 2 * dh).astype(jnp.float32)
    o = o * lax.rsqrt(jnp.mean(jnp.square(o), axis=-1, keepdims=True) + LN_EPS) * gain.astype(jnp.float32)
    return (o * (1.0 - lam_init)).astype(v.dtype)


def dilated_branch(q, k, v, table, window, dilation):
    B, S, H, dh = q.shape
    L = S // dilation
    R = window // 2 // dilation
    nb = -(-L // R)
    Lp = nb * R

    def to_strided(t):
        t = t.reshape(B, L, dilation, H, dh).transpose(0, 2, 1, 3, 4)
        return jnp.pad(t, ((0, 0), (0, 0), (0, Lp - L), (0, 0), (0, 0)))

    def windows(t):
        tp = jnp.pad(to_strided(t), ((0, 0), (0, 0), (R, R), (0, 0), (0, 0)))
        tp = tp.reshape(B, dilation, nb + 2, R, H, dh)
        return jnp.concatenate([tp[:, :, :-2], tp[:, :, 1:-1], tp[:, :, 2:]], axis=3)

    qs = to_strided(q).reshape(B, dilation, nb, R, H, dh)
    kw, vw = windows(k), windows(v)
    qi = jnp.arange(nb)[:, None] * R + jnp.arange(R)[None, :]
    ki = jnp.arange(nb)[:, None] * R - R + jnp.arange(3 * R)[None, :]
    rel = ki[:, None, :] - qi[:, :, None]
    valid = (jnp.abs(rel) <= R) & (ki[:, None, :] >= 0) & (ki[:, None, :] < L)
    bias = t5_bias(rel[0] * dilation, table)
    s = jnp.einsum('brnqhc,brnkhc->brnhqk', qs, kw).astype(jnp.float32) * dh ** -0.5
    s = jnp.where(valid[None, None, :, None], s + bias[None, None, None], -jnp.inf)
    m = jnp.max(s, axis=-1)
    e = jnp.exp(s - m[..., None])
    den = jnp.sum(e, axis=-1)
    num = jnp.einsum('brnhqk,brnkhc->brnqhc', e, vw.astype(jnp.float32))
    num = num.reshape(B, dilation, Lp, H, dh)[:, :, :L].transpose(0, 2, 1, 3, 4).reshape(B, S, H, dh)

    def back(t):
        t = jnp.moveaxis(t, 3, 4).reshape(B, dilation, Lp, H)[:, :, :L]
        return t.transpose(0, 2, 1, 3).reshape(B, S, H)

    return num, back(den), back(m)


def dilated_attention(q, k, v, table):
    outs = [dilated_branch(q, k, v, table, w, d) for (w, d) in DIL_BRANCHES]
    m_all = outs[0][2]
    for o in outs[1:]:
        m_all = jnp.maximum(m_all, o[2])
    num = sum(o[0] * jnp.exp(o[2] - m_all)[..., None] for o in outs)
    den = sum(o[1] * jnp.exp(o[2] - m_all) for o in outs)
    return (num / den[..., None]).astype(q.dtype)


def neighborhood_attention(q, k, v, rpb):
    B, S, H, dh = q.shape
    rows = S // GRID_W
    kh = min(NA_KH, rows)
    qg = q.reshape(B, rows, GRID_W, H, dh)
    kg = k.reshape(B, rows, GRID_W, H, dh)
    vg = v.reshape(B, rows, GRID_W, H, dh)
    col = jnp.arange(GRID_W)
    cs = jnp.clip(col - NA_KW // 2, 0, GRID_W - NA_KW)
    col_mask = (col[None, :] >= cs[:, None]) & (col[None, :] < cs[:, None] + NA_KW)
    col_idx = jnp.clip(col[None, :] - col[:, None], -(NA_KW - 1), NA_KW - 1) + NA_KW - 1

    def one_row(r):
        rs = jnp.clip(r - kh // 2, 0, rows - kh)
        kr = lax.dynamic_slice_in_dim(kg, rs, kh, axis=1)
        vr = lax.dynamic_slice_in_dim(vg, rs, kh, axis=1)
        qr = lax.dynamic_index_in_dim(qg, r, axis=1, keepdims=False)
        row_idx = rs + jnp.arange(kh) - r + NA_KH - 1
        bias = rpb[:, row_idx[None, :, None], col_idx[:, None, :]].astype(jnp.float32)
        s = jnp.einsum('bqhc,bikhc->bhqik', qr, kr).astype(jnp.float32) * dh ** -0.5
        s = jnp.where(col_mask[:, None, :], s + bias[None], -jnp.inf)
        p = jax.nn.softmax(s.reshape(B, H, GRID_W, kh * GRID_W), axis=-1).reshape(s.shape)
        return jnp.einsum('bhqik,bikhc->bqhc', p.astype(v.dtype), vr)

    o = lax.map(one_row, jnp.arange(rows))
    return jnp.moveaxis(o, 0, 1).reshape(B, S, H, dh)


def even_mixer(x, w_in, table, diff_lambda, diff_gain, w_out, layer_idx):
    B, S, _ = x.shape
    da = N_HEADS_DIFF * 2 * HEAD_DIM
    db = N_HEADS_DIL * HEAD_DIM
    h = x @ w_in
    qa, ka, va, qb, kb, vb = jnp.split(h, [da, 2 * da, 3 * da, 3 * da + db, 3 * da + 2 * db], axis=-1)
    qa = qa.reshape(B, S, N_HEADS_DIFF, 2, HEAD_DIM)
    ka = ka.reshape(B, S, N_HEADS_DIFF, 2, HEAD_DIM)
    va = va.reshape(B, S, N_HEADS_DIFF, 2 * HEAD_DIM)
    qb = qb.reshape(B, S, N_HEADS_DIL, HEAD_DIM)
    kb = kb.reshape(B, S, N_HEADS_DIL, HEAD_DIM)
    vb = vb.reshape(B, S, N_HEADS_DIL, HEAD_DIM)
    lam_init = 0.8 - 0.6 * math.exp(-0.3 * layer_idx)
    dl = diff_lambda.astype(jnp.float32)
    lam = jnp.exp(jnp.sum(dl[0] * dl[1])) - jnp.exp(jnp.sum(dl[2] * dl[3])) + lam_init
    ya = diff_attention(qa, ka, va, table[:, :N_HEADS_DIFF], lam, diff_gain, lam_init).reshape(B, S, -1)
    yb = dilated_attention(qb, kb, vb, table[:, N_HEADS_DIFF:]).reshape(B, S, -1)
    return jnp.concatenate([ya, yb], axis=-1) @ w_out


def odd_mixer(x, w_in, rpb, w_out):
    B, S, _ = x.shape
    q, k, v = jnp.split(x @ w_in, 3, axis=-1)
    q = q.reshape(B, S, N_HEADS_NA, HEAD_DIM)
    k = k.reshape(B, S, N_HEADS_NA, HEAD_DIM)
    v = v.reshape(B, S, N_HEADS_NA, HEAD_DIM)
    return neighborhood_attention(q, k, v, rpb).reshape(B, S, -1) @ w_out


def moe(xb, w_router, router_bias, w1, w3, w2, ws1, ws3, ws2):
    N = xb.shape[0]
    scores = jax.nn.sigmoid((xb @ w_router).astype(jnp.float32))
    sel = scores + router_bias.astype(jnp.float32)
    grp_score = jnp.sum(lax.top_k(sel.reshape(N, N_GROUPS, N_EXPERTS // N_GROUPS), 2)[0], axis=-1)
    _, gidx = lax.top_k(grp_score, TOPK_GROUPS)
    gmask = jnp.sum(jax.nn.one_hot(gidx, N_GROUPS, dtype=jnp.float32), axis=-2) > 0
    sel = jnp.where(jnp.repeat(gmask, N_EXPERTS // N_GROUPS, axis=-1), sel, -jnp.inf)
    _, eidx = lax.top_k(sel, TOP_K)
    w = jnp.take_along_axis(scores, eidx, axis=-1)
    w = w / jnp.sum(w, axis=-1, keepdims=True) * ROUTED_SCALE
    gates = jnp.einsum('nk,nke->ne', w, jax.nn.one_hot(eidx, N_EXPERTS, dtype=jnp.float32))
    h = jax.nn.silu(jnp.einsum('nd,edf->nef', xb, w1)) * jnp.einsum('nd,edf->nef', xb, w3)
    routed = jnp.einsum('nef,efd->nd', h * gates[..., None].astype(h.dtype), w2)
    shared = (jax.nn.silu(xb @ ws1) * (xb @ ws3)) @ ws2
    return routed + shared


def setup_inputs(seed: int = 0) -> dict:
    key = jax.random.key(seed)
    ks = jax.random.split(key, 21)
    D = D_MODEL

    def nrm(k, shape, scale):
        return jax.random.normal(k, shape, jnp.float32) * scale

    return {
        'x': nrm(ks[0], (BATCH, SEQ, D), 1.0),
        't5_table': nrm(ks[1], (T5_BUCKETS, N_HEADS_DIFF + N_HEADS_DIL), 0.5),
        'w_in_even': nrm(ks[2], (N_EVEN, D, D_IN_EVEN), D ** -0.5),
        'diff_lambda': nrm(ks[3], (N_EVEN, 4, HEAD_DIM), 0.1),
        'diff_gain': 1.0 + nrm(ks[4], (N_EVEN, 2 * HEAD_DIM), 0.02),
        'w_out_even': nrm(ks[5], (N_EVEN, D, D), BETA * D ** -0.5),
        'w_in_odd': nrm(ks[6], (N_ODD, D, D_IN_ODD), D ** -0.5),
        'na_rpb': nrm(ks[7], (N_ODD, N_HEADS_NA, 2 * NA_KH - 1, 2 * NA_KW - 1), 0.5),
        'w_out_odd': nrm(ks[8], (N_ODD, D, D), BETA * D ** -0.5),
        'ln_mix_g': 1.0 + nrm(ks[9], (DEPTH, D), 0.02),
        'ln_mix_b': nrm(ks[10], (DEPTH, D), 0.02),
        'router_w': nrm(ks[11], (DEPTH, D, N_EXPERTS), D ** -0.5),
        'router_bias': nrm(ks[12], (DEPTH, N_EXPERTS), 0.01),
        'exp_w1': nrm(ks[13], (DEPTH, N_EXPERTS, D, EXPERT_DIM), D ** -0.5),
        'exp_w3': nrm(ks[14], (DEPTH, N_EXPERTS, D, EXPERT_DIM), D ** -0.5),
        'exp_w2': nrm(ks[15], (DEPTH, N_EXPERTS, EXPERT_DIM, D), BETA * EXPERT_DIM ** -0.5),
        'sh_w1': nrm(ks[16], (DEPTH, D, SHARED_DIM), D ** -0.5),
        'sh_w3': nrm(ks[17], (DEPTH, D, SHARED_DIM), D ** -0.5),
        'sh_w2': nrm(ks[18], (DEPTH, SHARED_DIM, D), BETA * SHARED_DIM ** -0.5),
        'ln_ffn_g': 1.0 + nrm(ks[19], (DEPTH, D), 0.02),
        'ln_ffn_b': nrm(ks[20], (DEPTH, D), 0.02),
    }


def reference(x, t5_table, w_in_even, diff_lambda, diff_gain, w_out_even, w_in_odd, na_rpb, w_out_odd,
              ln_mix_g, ln_mix_b, router_w, router_bias, exp_w1, exp_w3, exp_w2, sh_w1, sh_w3, sh_w2,
              ln_ffn_g, ln_ffn_b):
    for l in range(DEPTH):
        j = l // 2
        if l % 2 == 0:
            y = even_mixer(x, w_in_even[j], t5_table, diff_lambda[j], diff_gain[j], w_out_even[j], l)
        else:
            y = odd_mixer(x, w_in_odd[j], na_rpb[j], w_out_odd[j])
        x = layer_norm(ALPHA * x + y, ln_mix_g[l], ln_mix_b[l])
        y = lax.map(lambda xb: moe(xb, router_w[l], router_bias[l], exp_w1[l], exp_w3[l], exp_w2[l],
                                   sh_w1[l], sh_w3[l], sh_w2[l]), x)
        x = layer_norm(ALPHA * x + y, ln_ffn_g[l], ln_ffn_b[l])
    return x
```

```python
import functools
import math

import jax
import jax.numpy as jnp
from jax import lax
from jax.experimental import pallas as pl
from jax.experimental.pallas import tpu as pltpu

HEAD_DIM = 128
DIL_BRANCHES = ((128, 1), (512, 4), (2048, 16))
GRID_W = 64
NA_KH = 8
NA_KW = 16
T5_BUCKETS = 32
T5_MAX_DIST = 1024
N_GROUPS = 8
TOPK_GROUPS = 4
TOP_K = 8
ROUTED_SCALE = 2.5
LN_EPS = 1e-5
NEG = -1e30

V7X_VMEM_LIMIT = 56 * 1024 * 1024

F32 = jnp.float32
BF16 = jnp.bfloat16


def _params(sem, vmem=None):
    return pltpu.CompilerParams(dimension_semantics=sem, vmem_limit_bytes=vmem)


def _fit(tile, dim):
    if dim <= tile:
        return dim
    return max(t for t in range(128, tile + 1, 128) if dim % t == 0)


def _nt_dot(a, b, **kw):
    return lax.dot_general(a, b, (((1,), (1,)), ((), ())), preferred_element_type=F32, **kw)


def _layer_norm(z, g, b):
    mu = jnp.mean(z, axis=-1, keepdims=True)
    zc = z - mu
    var = jnp.mean(zc * zc, axis=-1, keepdims=True)
    return zc * lax.rsqrt(var + LN_EPS) * g + b


def _mm_body(x_ref, w_ref, o_ref):
    o_ref[...] = jnp.dot(x_ref[...], w_ref[...], preferred_element_type=F32).astype(o_ref.dtype)


def matmul(x, w, out_dtype, tm=512, tn=1024):
    m, k = x.shape
    n = w.shape[1]
    tm, tn = _fit(tm, m), _fit(tn, n)
    return pl.pallas_call(
        _mm_body,
        grid=(n // tn, m // tm),
        in_specs=[pl.BlockSpec((tm, k), lambda j, i: (i, 0)),
                  pl.BlockSpec((k, tn), lambda j, i: (0, j))],
        out_specs=pl.BlockSpec((tm, tn), lambda j, i: (i, j)),
        out_shape=jax.ShapeDtypeStruct((m, n), out_dtype),
        compiler_params=_params(("parallel", "parallel"), V7X_VMEM_LIMIT),
        name="in_proj",
    )(x, w)


def _proj_ln_body(a_ref, w_ref, res_ref, g_ref, b_ref, of_ref, ob_ref, acc_ref, *, alpha, nk):
    k = pl.program_id(1)

    @pl.when(k == 0)
    def _():
        acc_ref[...] = jnp.zeros_like(acc_ref)

    acc_ref[...] += jnp.dot(a_ref[...], w_ref[...], preferred_element_type=F32)

    @pl.when(k == nk - 1)
    def _():
        y = _layer_norm(alpha * res_ref[...] + acc_ref[...], g_ref[...], b_ref[...])
        of_ref[...] = y
        ob_ref[...] = y.astype(BF16)


def proj_res_ln(a, w, res, g, b, alpha, tm=256, tk=512):
    m, kdim = a.shape
    d = w.shape[1]
    tm, tk = _fit(tm, m), _fit(tk, kdim)
    nk = kdim // tk
    return pl.pallas_call(
        functools.partial(_proj_ln_body, alpha=alpha, nk=nk),
        grid=(m // tm, nk),
        in_specs=[pl.BlockSpec((tm, tk), lambda i, k: (i, k)),
                  pl.BlockSpec((tk, d), lambda i, k: (k, 0)),
                  pl.BlockSpec((tm, d), lambda i, k: (i, 0)),
                  pl.BlockSpec((1, d), lambda i, k: (0, 0)),
                  pl.BlockSpec((1, d), lambda i, k: (0, 0))],
        out_specs=[pl.BlockSpec((tm, d), lambda i, k: (i, 0)),
                   pl.BlockSpec((tm, d), lambda i, k: (i, 0))],
        out_shape=[jax.ShapeDtypeStruct((m, d), F32), jax.ShapeDtypeStruct((m, d), BF16)],
        scratch_shapes=[pltpu.VMEM((tm, d), F32)],
        compiler_params=_params(("parallel", "arbitrary"), V7X_VMEM_LIMIT),
        name="out_proj_ln",
    )(a, w, res, g.reshape(1, d), b.reshape(1, d))


def _softmax_parts(s):
    m = jnp.max(s, axis=-1, keepdims=True)
    e = jnp.exp(s - m)
    return e, jnp.sum(e, axis=-1, keepdims=True)


def _band_bias(band_ref, i, nq):
    return jnp.concatenate([band_ref[nq - 1 - i + kb] for kb in range(nq)], axis=-1)


def _diff_body(dl_ref, gain_ref, q_ref, k_ref, v_ref, band_ref, o_ref, *, nq, lam_init, scale):
    i = pl.program_id(2)
    bias = _band_bias(band_ref, i, nq)
    dl = dl_ref[...]
    lam = (jnp.exp(jnp.sum(dl[0:1] * dl[1:2], axis=-1, keepdims=True))
           - jnp.exp(jnp.sum(dl[2:3] * dl[3:4], axis=-1, keepdims=True)) + lam_init)
    q = q_ref[...]
    k = k_ref[...]
    e0, l0 = _softmax_parts(_nt_dot(q[:, :HEAD_DIM], k[:, :HEAD_DIM]) * scale + bias)
    e1, l1 = _softmax_parts(_nt_dot(q[:, HEAD_DIM:], k[:, HEAD_DIM:]) * scale + bias)
    a = e0 * (1.0 / l0) - e1 * (lam / l1)
    o = jnp.dot(a.astype(BF16), v_ref[...], preferred_element_type=F32)
    o = o * lax.rsqrt(jnp.mean(o * o, axis=-1, keepdims=True) + LN_EPS) * gain_ref[...]
    o_ref[...] = (o * (1.0 - lam_init)).astype(o_ref.dtype)


def _dil_body(q_ref, k_ref, v_ref, band_ref, o_ref, *, nq, scale):
    i = pl.program_id(2)
    bias = _band_bias(band_ref, i, nq)
    e, l = _softmax_parts(_nt_dot(q_ref[...], k_ref[...]) * scale + bias)
    o = jnp.dot(e.astype(BF16), v_ref[...], preferred_element_type=F32)
    o_ref[...] = (o * (1.0 / l)).astype(o_ref.dtype)


def _na_body(q_ref, k_ref, v_ref, bias_ref, o_ref, *, nrb, win, scale):
    rb = pl.program_id(2)
    rows_q = q_ref.shape[0] // GRID_W
    rows_all = k_ref.shape[0] // GRID_W
    w0 = jnp.clip(rb * rows_q - NA_KH // 2, 0, rows_all - win)
    start = pl.multiple_of(w0 * GRID_W, GRID_W)
    k = k_ref[pl.ds(start, win * GRID_W), :]
    v = v_ref[pl.ds(start, win * GRID_W), :]
    e, l = _softmax_parts(_nt_dot(q_ref[...], k) * scale + bias_ref[...])
    o = jnp.dot(e.astype(BF16), v, preferred_element_type=F32)
    o_ref[...] = (o * (1.0 / l)).astype(o_ref.dtype)


def _t5_rel_table(table, s):
    rel = jnp.arange(-(s - 1), s, dtype=jnp.int32)
    half = T5_BUCKETS // 2
    exact = half // 2
    n = jnp.abs(rel)
    log_ratio = jnp.log(jnp.maximum(n, 1).astype(F32) / exact) / math.log(T5_MAX_DIST / exact)
    large = jnp.minimum(exact + (log_ratio * (half - exact)).astype(jnp.int32), half - 1)
    bucket = jnp.where(rel > 0, half, 0) + jnp.where(n < exact, n, large)
    return table[bucket].astype(F32)


def _toeplitz_tiles(rel_tab, s, tq):
    nq = s // tq
    j = jnp.arange(2 * nq - 1, dtype=jnp.int32)[:, None, None]
    qq = jnp.arange(tq, dtype=jnp.int32)[None, :, None]
    kk = jnp.arange(tq, dtype=jnp.int32)[None, None, :]
    idx = (j - (nq - 1)) * tq + kk - qq + (s - 1)
    return jnp.moveaxis(rel_tab[idx], -1, 0)


def _dilated_log_multiplicity(s):
    rel = jnp.arange(-(s - 1), s, dtype=jnp.int32)
    cnt = jnp.zeros(rel.shape, F32)
    for window, dilation in DIL_BRANCHES:
        reach = (window // 2 // dilation) * dilation
        cnt = cnt + ((rel % dilation == 0) & (jnp.abs(rel) <= reach)).astype(F32)
    return jnp.where(cnt > 0, jnp.log(jnp.maximum(cnt, 1.0)), NEG)


def even_attention(h, bsz, s, d, t5_table, diff_lambda, diff_gain, lam_init, tq=256):
    n = bsz * s
    dh = HEAD_DIM
    h_diff = d // (4 * dh)
    h_dil = d // (2 * dh)
    nq = s // tq
    scale = dh ** -0.5
    rel_tab = _t5_rel_table(t5_table, s)
    band_a = _toeplitz_tiles(rel_tab[:, :h_diff], s, tq)
    band_b = _toeplitz_tiles(rel_tab[:, h_diff:] + _dilated_log_multiplicity(s)[:, None], s, tq)
    nb = 2 * nq - 1

    ya = pl.pallas_call(
        functools.partial(_diff_body, nq=nq, lam_init=lam_init, scale=scale),
        grid=(bsz, h_diff, nq),
        in_specs=[pl.BlockSpec((4, dh), lambda b, hh, i: (0, 0)),
                  pl.BlockSpec((1, 2 * dh), lambda b, hh, i: (0, 0)),
                  pl.BlockSpec((tq, 2 * dh), lambda b, hh, i: (b * nq + i, hh)),
                  pl.BlockSpec((s, 2 * dh), lambda b, hh, i: (b, h_diff + hh)),
                  pl.BlockSpec((s, 2 * dh), lambda b, hh, i: (b, 2 * h_diff + hh)),
                  pl.BlockSpec((None, nb, tq, tq), lambda b, hh, i: (hh, 0, 0, 0))],
        out_specs=pl.BlockSpec((tq, 2 * dh), lambda b, hh, i: (b * nq + i, hh)),
        out_shape=jax.ShapeDtypeStruct((n, h_diff * 2 * dh), BF16),
        compiler_params=_params(("parallel", "parallel", "parallel"), V7X_VMEM_LIMIT),
        name="diff_attn",
    )(diff_lambda.astype(F32), diff_gain.astype(F32).reshape(1, 2 * dh), h, h, h, band_a)

    c0 = 3 * h_diff * 2
    yb = pl.pallas_call(
        functools.partial(_dil_body, nq=nq, scale=scale),
        grid=(bsz, h_dil, nq),
        in_specs=[pl.BlockSpec((tq, dh), lambda b, hh, i: (b * nq + i, c0 + hh)),
                  pl.BlockSpec((s, dh), lambda b, hh, i: (b, c0 + h_dil + hh)),
                  pl.BlockSpec((s, dh), lambda b, hh, i: (b, c0 + 2 * h_dil + hh)),
                  pl.BlockSpec((None, nb, tq, tq), lambda b, hh, i: (hh, 0, 0, 0))],
        out_specs=pl.BlockSpec((tq, dh), lambda b, hh, i: (b * nq + i, hh)),
        out_shape=jax.ShapeDtypeStruct((n, h_dil * dh), BF16),
        compiler_params=_params(("parallel", "parallel", "parallel"), V7X_VMEM_LIMIT),
        name="dilated_attn",
    )(h, h, h, band_b)
    return jnp.concatenate([ya, yb], axis=-1)


def _na_bias_tiles(rpb, rows, rows_q, win):
    w = GRID_W
    kh = min(NA_KH, rows)
    col = jnp.arange(w, dtype=jnp.int32)
    cs = jnp.clip(col - NA_KW // 2, 0, w - NA_KW)
    col_mask = (col[None, :] >= cs[:, None]) & (col[None, :] < cs[:, None] + NA_KW)
    col_idx = jnp.clip(col[None, :] - col[:, None], -(NA_KW - 1), NA_KW - 1) + NA_KW - 1
    cm = jnp.where(col_mask[None, None], rpb.astype(F32)[:, :, col_idx], NEG)
    masked = jnp.full((rpb.shape[0], w, w), NEG, F32)
    nrb = rows // rows_q
    cases = []
    for rb in (0, 1, nrb - 1):
        r0 = rb * rows_q
        w0 = min(max(r0 - NA_KH // 2, 0), rows - win)
        qrows = []
        for qr in range(rows_q):
            r = r0 + qr
            rs = min(max(r - kh // 2, 0), rows - kh)
            blocks = []
            for kj in range(win):
                kr = w0 + kj
                blocks.append(cm[:, kr - r + NA_KH - 1] if rs <= kr < rs + kh else masked)
            qrows.append(jnp.concatenate(blocks, axis=-1))
        cases.append(jnp.concatenate(qrows, axis=-2))
    return jnp.stack(cases, axis=1)


def odd_attention(h, bsz, s, d, rpb, rows_q=4, win=12):
    n = bsz * s
    dh = HEAD_DIM
    nh = d // dh
    rows = s // GRID_W
    nrb = rows // rows_q
    assert rows % rows_q == 0 and nrb >= 3 and rows >= win
    assert win >= rows_q + min(NA_KH, rows) - 1 and rows_q <= NA_KH // 2
    tq = rows_q * GRID_W
    bias = _na_bias_tiles(rpb, rows, rows_q, win)

    def case(rb):
        return jnp.where(rb == 0, 0, jnp.where(rb == nrb - 1, 2, 1))

    return pl.pallas_call(
        functools.partial(_na_body, nrb=nrb, win=win, scale=dh ** -0.5),
        grid=(nh, bsz, nrb),
        in_specs=[pl.BlockSpec((tq, dh), lambda hh, b, rb: (b * nrb + rb, hh)),
                  pl.BlockSpec((s, dh), lambda hh, b, rb: (b, nh + hh)),
                  pl.BlockSpec((s, dh), lambda hh, b, rb: (b, 2 * nh + hh)),
                  pl.BlockSpec((None, None, tq, win * GRID_W), lambda hh, b, rb: (hh, case(rb), 0, 0))],
        out_specs=pl.BlockSpec((tq, dh), lambda hh, b, rb: (b * nrb + rb, hh)),
        out_shape=jax.ShapeDtypeStruct((n, d), BF16),
        compiler_params=_params(("parallel", "parallel", "parallel")),
        name="na_attn",
    )(h, h, h, bias)


def _router_body(x_ref, wr_ref, rb_ref, tri_ref, eidx_ref, wts_ref, rank_ref, cnt_ref, carry_ref):
    ne, tm = wr_ref.shape[0], x_ref.shape[0]
    gsz = ne // N_GROUPS
    ninf = -jnp.inf

    @pl.when(pl.program_id(0) == 0)
    def _():
        carry_ref[...] = jnp.zeros_like(carry_ref)

    logits = _nt_dot(wr_ref[...], x_ref[...], precision=lax.Precision.HIGHEST)
    scores = jax.nn.sigmoid(logits)
    sel = scores + rb_ref[...]

    g3 = sel.reshape(N_GROUPS, gsz, tm)
    i_in = lax.broadcasted_iota(jnp.int32, g3.shape, 1)
    m1 = jnp.max(g3, axis=1, keepdims=True)
    first = jnp.min(jnp.where(g3 == m1, i_in, gsz), axis=1, keepdims=True)
    m2 = jnp.max(jnp.where(i_in == first, ninf, g3), axis=1, keepdims=True)
    grp = m1 + m2

    i_g = lax.broadcasted_iota(jnp.int32, grp.shape, 0)
    keep = jnp.zeros(grp.shape, jnp.bool_)
    for _ in range(TOPK_GROUPS):
        gm = jnp.max(grp, axis=0, keepdims=True)
        gi = jnp.min(jnp.where(grp == gm, i_g, N_GROUPS), axis=0, keepdims=True)
        hit = i_g == gi
        keep = keep | hit
        grp = jnp.where(hit, ninf, grp)
    selm = jnp.where(keep, g3, ninf).reshape(ne, tm)

    i_e = lax.broadcasted_iota(jnp.int32, (ne, tm), 0)
    hits, eids, ws = [], [], []
    for _ in range(TOP_K):
        mx = jnp.max(selm, axis=0, keepdims=True)
        ei = jnp.min(jnp.where(selm == mx, i_e, ne), axis=0, keepdims=True)
        hit = i_e == ei
        ws.append(jnp.sum(jnp.where(hit, scores, 0.0), axis=0, keepdims=True))
        selm = jnp.where(hit, ninf, selm)
        hits.append(hit)
        eids.append(ei)
    wsum = ws[0]
    for wk in ws[1:]:
        wsum = wsum + wk

    msel = jnp.zeros((ne, tm), F32)
    for hit in hits:
        msel = msel + hit.astype(F32)
    incl = jnp.dot(msel.astype(BF16), tri_ref[...], preferred_element_type=F32)
    rank_ex = incl - msel + carry_ref[:, 0:1]
    ranks = [jnp.sum(jnp.where(hit, rank_ex, 0.0), axis=0, keepdims=True) for hit in hits]

    eidx_ref[...] = jnp.concatenate(eids, axis=0)
    wts_ref[...] = jnp.concatenate([wk / wsum * ROUTED_SCALE for wk in ws], axis=0)
    rank_ref[...] = jnp.concatenate(ranks, axis=0).astype(jnp.int32)
    carry_ref[...] = carry_ref[...] + jnp.sum(msel, axis=1, keepdims=True)
    cnt_ref[...] = carry_ref[...]


def router(x, w_router, router_bias, tm=512):
    n, d = x.shape
    ne = w_router.shape[1]
    tm = min(tm, n)
    tri = (jnp.arange(tm)[:, None] <= jnp.arange(tm)[None, :]).astype(BF16)
    kn = jax.ShapeDtypeStruct((TOP_K, n), jnp.int32)
    eidx, wts, rank, cnt = pl.pallas_call(
        _router_body,
        grid=(n // tm,),
        in_specs=[pl.BlockSpec((tm, d), lambda t: (t, 0)),
                  pl.BlockSpec((ne, d), lambda t: (0, 0)),
                  pl.BlockSpec((ne, 1), lambda t: (0, 0)),
                  pl.BlockSpec((tm, tm), lambda t: (0, 0))],
        out_specs=[pl.BlockSpec((TOP_K, tm), lambda t: (0, t)),
                   pl.BlockSpec((TOP_K, tm), lambda t: (0, t)),
                   pl.BlockSpec((TOP_K, tm), lambda t: (0, t)),
                   pl.BlockSpec((ne, 128), lambda t: (0, 0))],
        out_shape=[kn, jax.ShapeDtypeStruct((TOP_K, n), F32), kn,
                   jax.ShapeDtypeStruct((ne, 128), F32)],
        scratch_shapes=[pltpu.VMEM((ne, 128), F32)],
        compiler_params=_params(("arbitrary",)),
        name="router",
    )(x, w_router.T.astype(F32), router_bias.astype(F32).reshape(ne, 1), tri)
    return eidx, wts, rank, cnt[:, 0].astype(jnp.int32)


def _row_copy(src, src_row, dst, dst_row, sem):
    return pltpu.make_async_copy(src.at[pl.ds(src_row, 1), :], dst.at[pl.ds(dst_row, 1), :], sem)


def _dispatch_body(pos_ref, x_ref, xs_hbm, sem):
    tm = x_ref.shape[0]
    for k in range(TOP_K):
        def issue(r, c, k=k):
            _row_copy(x_ref, r, xs_hbm, pos_ref[k, r], sem).start()
            return c
        lax.fori_loop(0, tm, issue, 0)

    def drain(r, c):
        _row_copy(x_ref, 0, xs_hbm, 0, sem).wait()
        return c
    lax.fori_loop(0, TOP_K * tm, drain, 0)


def dispatch(x, pos_tiles):
    n, d = x.shape
    nt, _, tm = pos_tiles.shape
    return pl.pallas_call(
        _dispatch_body,
        grid=(nt,),
        in_specs=[pl.BlockSpec((None, TOP_K, tm), lambda i: (i, 0, 0), memory_space=pltpu.SMEM),
                  pl.BlockSpec((tm, d), lambda i: (i, 0))],
        out_specs=pl.BlockSpec(memory_space=pl.ANY),
        out_shape=jax.ShapeDtypeStruct((TOP_K * n, d), x.dtype),
        scratch_shapes=[pltpu.SemaphoreType.DMA],
        compiler_params=_params(("arbitrary",)),
        name="dispatch",
    )(pos_tiles, x)


def _expert_body(tile_ref, exp_ref, lo_ref, hi_ref, first_ref, xs_ref, w13_ref, w2_ref, ys_ref):
    it = pl.program_id(0)
    tm = xs_ref.shape[0]
    f = w2_ref.shape[0]
    lo, hi = lo_ref[it], hi_ref[it]

    @pl.when(hi > lo)
    def _():
        a = jnp.dot(xs_ref[...].astype(BF16), w13_ref[...], preferred_element_type=F32)
        g = jax.nn.silu(a[:, :f]) * a[:, f:]
        row = tile_ref[it] * tm + lax.broadcasted_iota(jnp.int32, (tm, 1), 0)
        g = jnp.where((row >= lo) & (row < hi), g, 0.0)
        y = jnp.dot(g.astype(BF16), w2_ref[...], preferred_element_type=F32)

        @pl.when(first_ref[it] == 1)
        def _():
            ys_ref[...] = y

        @pl.when(first_ref[it] == 0)
        def _():
            ys_ref[...] += y


def expert_ffn(xs, w13, w2, items, tm):
    p, d = xs.shape
    f = w2.shape[1]
    n_items = items[0].shape[0]
    grid_spec = pltpu.PrefetchScalarGridSpec(
        num_scalar_prefetch=5,
        grid=(n_items,),
        in_specs=[pl.BlockSpec((tm, d), lambda i, t, e, lo, hi, fi: (t[i], 0)),
                  pl.BlockSpec((None, d, 2 * f), lambda i, t, e, lo, hi, fi: (e[i], 0, 0)),
                  pl.BlockSpec((None, f, d), lambda i, t, e, lo, hi, fi: (e[i], 0, 0))],
        out_specs=pl.BlockSpec((tm, d), lambda i, t, e, lo, hi, fi: (t[i], 0)),
    )
    return pl.pallas_call(
        _expert_body,
        grid_spec=grid_spec,
        out_shape=jax.ShapeDtypeStruct((p, d), F32),
        compiler_params=_params(("arbitrary",), V7X_VMEM_LIMIT),
        name="expert_ffn",
    )(*items, xs, w13, w2)


def _work_items(counts, p, tm):
    ne = counts.shape[0]
    nt = p // tm
    n_items = nt + ne - 1
    ends = jnp.cumsum(counts)
    starts = ends - counts
    t_first = starts // tm
    t_last = jnp.maximum(ends - 1, 0) // tm
    per = jnp.where(counts > 0, t_last - t_first + 1, 0)
    item_end = jnp.cumsum(per)
    item_start = item_end - per
    total = item_end[-1]
    i = jnp.arange(n_items, dtype=jnp.int32)
    ic = jnp.minimum(i, total - 1)
    e = jnp.sum((item_end[None, :] <= ic[:, None]).astype(jnp.int32), axis=1)
    tile = t_first[e] + ic - item_start[e]
    lo = jnp.maximum(starts[e], tile * tm)
    hi = jnp.minimum(ends[e], (tile + 1) * tm)
    live = i < total
    lo = jnp.where(live, lo, 0)
    hi = jnp.where(live, hi, 0)
    first = (live & (lo == tile * tm)).astype(jnp.int32)
    as_i32 = lambda a: a.astype(jnp.int32)
    return as_i32(tile), as_i32(e), as_i32(lo), as_i32(hi), first


def _combine_body(pos_ref, x_ref, xb_ref, wt_ref, ws13_ref, ws2_ref, g_ref, b_ref, ys_hbm,
                  of_ref, ob_ref, buf, acc_ref, sems, *, alpha):
    tm = x_ref.shape[0]
    f = ws2_ref.shape[0]

    def issue(k, slot):
        def body(r, c):
            _row_copy(ys_hbm, pos_ref[k, r], buf.at[slot], r, sems.at[slot]).start()
            return c
        lax.fori_loop(0, tm, body, 0)

    def drain(slot):
        def body(r, c):
            _row_copy(ys_hbm, 0, buf.at[slot], 0, sems.at[slot]).wait()
            return c
        lax.fori_loop(0, tm, body, 0)

    issue(0, 0)
    xb = xb_ref[...]
    a = jnp.dot(xb, ws13_ref[...], preferred_element_type=F32)
    hs = jax.nn.silu(a[:, :f]) * a[:, f:]
    acc_ref[...] = alpha * x_ref[...] + jnp.dot(hs.astype(BF16), ws2_ref[...],
                                                preferred_element_type=F32)
    wt = wt_ref[...]
    for k in range(TOP_K):
        slot = k % 2
        if k + 1 < TOP_K:
            issue(k + 1, 1 - slot)
        drain(slot)
        acc_ref[...] += wt[:, k:k + 1] * buf[slot]
    y = _layer_norm(acc_ref[...], g_ref[...], b_ref[...])
    of_ref[...] = y
    ob_ref[...] = y.astype(BF16)


def combine_shared_ln(x, xb, ys, pos_tiles, wts_t, ws13, ws2, g, b, alpha):
    n, d = x.shape
    nt, _, tm = pos_tiles.shape
    f = ws2.shape[0]
    tok = lambda i: (i, 0)
    const = lambda i: (0, 0)
    return pl.pallas_call(
        functools.partial(_combine_body, alpha=alpha),
        grid=(nt,),
        in_specs=[pl.BlockSpec((None, TOP_K, tm), lambda i: (i, 0, 0), memory_space=pltpu.SMEM),
                  pl.BlockSpec((tm, d), tok),
                  pl.BlockSpec((tm, d), tok),
                  pl.BlockSpec((tm, TOP_K), tok),
                  pl.BlockSpec((d, 2 * f), const),
                  pl.BlockSpec((f, d), const),
                  pl.BlockSpec((1, d), const),
                  pl.BlockSpec((1, d), const),
                  pl.BlockSpec(memory_space=pl.ANY)],
        out_specs=[pl.BlockSpec((tm, d), tok), pl.BlockSpec((tm, d), tok)],
        out_shape=[jax.ShapeDtypeStruct((n, d), F32), jax.ShapeDtypeStruct((n, d), BF16)],
        scratch_shapes=[pltpu.VMEM((2, tm, d), F32), pltpu.VMEM((tm, d), F32),
                        pltpu.SemaphoreType.DMA((2,))],
        compiler_params=_params(("arbitrary",), V7X_VMEM_LIMIT),
        name="combine_ln",
    )(pos_tiles, x, xb, wts_t, ws13, ws2, g.reshape(1, d), b.reshape(1, d), ys)


def moe_block(x, xb, w_router, router_bias, w1, w3, w2, ws1, ws3, ws2, g, b, alpha,
              tm_tok=128, tm_exp=256):
    n, d = x.shape
    tm_tok, tm_exp = min(tm_tok, n), min(tm_exp, n)
    eidx, wts, rank, counts = router(x, w_router, router_bias)
    starts = jnp.cumsum(counts) - counts
    pos = starts[eidx] + rank
    pos_tiles = pos.reshape(TOP_K, n // tm_tok, tm_tok).transpose(1, 0, 2)
    xs = dispatch(x, pos_tiles)
    w13 = jnp.concatenate([w1.astype(BF16), w3.astype(BF16)], axis=-1)
    items = _work_items(counts, TOP_K * n, tm_exp)
    ys = expert_ffn(xs, w13, w2.astype(BF16), items, tm_exp)
    ws13 = jnp.concatenate([ws1.astype(BF16), ws3.astype(BF16)], axis=-1)
    return combine_shared_ln(x, xb, ys, pos_tiles, wts.T, ws13, ws2.astype(BF16), g, b, alpha)


def kernel(x, t5_table, w_in_even, diff_lambda, diff_gain, w_out_even, w_in_odd, na_rpb, w_out_odd,
           ln_mix_g, ln_mix_b, router_w, router_bias, exp_w1, exp_w3, exp_w2, sh_w1, sh_w3, sh_w2,
           ln_ffn_g, ln_ffn_b):
    bsz, s, d = x.shape
    depth = ln_mix_g.shape[0]
    alpha = (2 * depth) ** 0.25
    xf = x.reshape(bsz * s, d).astype(F32)
    xb = xf.astype(BF16)
    for l in range(depth):
        j = l // 2
        if l % 2 == 0:
            h = matmul(xb, w_in_even[j].astype(BF16), BF16)
            lam_init = 0.8 - 0.6 * math.exp(-0.3 * l)
            att = even_attention(h, bsz, s, d, t5_table, diff_lambda[j], diff_gain[j], lam_init)
            w_out = w_out_even[j]
        else:
            h = matmul(xb, w_in_odd[j].astype(BF16), BF16)
            att = odd_attention(h, bsz, s, d, na_rpb[j])
            w_out = w_out_odd[j]
        xf, xb = proj_res_ln(att, w_out.astype(BF16), xf, ln_mix_g[l], ln_mix_b[l], alpha)
        xf, xb = moe_block(xf, xb, router_w[l], router_bias[l], exp_w1[l], exp_w3[l], exp_w2[l],
                           sh_w1[l], sh_w3[l], sh_w2[l], ln_ffn_g[l], ln_ffn_b[l], alpha)
    return xf.reshape(bsz, s, d).astype(x.dtype)
```

```python
import functools
import math

import jax
import jax.numpy as jnp
from jax import lax
from jax.experimental import pallas as pl
from jax.experimental.pallas import tpu as pltpu

HEAD_DIM = 128
DIL_BRANCHES = ((128, 1), (512, 4), (2048, 16))
GRID_W = 64
NA_KH = 8
NA_KW = 16
T5_BUCKETS = 32
T5_MAX_DIST = 1024
N_GROUPS = 8
TOPK_GROUPS = 4
TOP_K = 8
ROUTED_SCALE = 2.5
LN_EPS = 1e-5
NEG = -1e30
LOG2E = math.log2(math.e)

V7X_VMEM_LIMIT = 56 * 1024 * 1024
LANES = 128
EXPERT_OUT_CHUNK = 512
DMA_UNROLL = 8

F32 = jnp.float32
BF16 = jnp.bfloat16


def _params(sem, vmem=None):
    return pltpu.CompilerParams(dimension_semantics=sem, vmem_limit_bytes=vmem)


def _fit(tile, dim):
    if dim <= tile:
        return dim
    return max(t for t in range(128, tile + 1, 128) if dim % t == 0)


def _nt_dot(a, b, **kw):
    return lax.dot_general(a, b, (((1,), (1,)), ((), ())), preferred_element_type=F32, **kw)


def _layer_norm(z, g, b):
    mu = jnp.mean(z, axis=-1, keepdims=True)
    zc = z - mu
    var = jnp.mean(zc * zc, axis=-1, keepdims=True)
    return zc * lax.rsqrt(var + LN_EPS) * g + b


def _mm_body(x_ref, w_ref, o_ref):
    o_ref[...] = jnp.dot(x_ref[...], w_ref[...], preferred_element_type=F32).astype(o_ref.dtype)


def matmul(x, w, out_dtype, tm=512, tn=1024):
    m, k = x.shape
    n = w.shape[1]
    tm, tn = _fit(tm, m), _fit(tn, n)
    return pl.pallas_call(
        _mm_body,
        grid=(n // tn, m // tm),
        in_specs=[pl.BlockSpec((tm, k), lambda j, i: (i, 0)),
                  pl.BlockSpec((k, tn), lambda j, i: (0, j))],
        out_specs=pl.BlockSpec((tm, tn), lambda j, i: (i, j)),
        out_shape=jax.ShapeDtypeStruct((m, n), out_dtype),
        compiler_params=_params(("parallel", "parallel"), V7X_VMEM_LIMIT),
        name="in_proj",
    )(x, w)


def _proj_ln_body(a_ref, w_ref, res_ref, g_ref, b_ref, of_ref, ob_ref, acc_ref, *, alpha, nk):
    k = pl.program_id(1)

    @pl.when(k == 0)
    def _():
        acc_ref[...] = jnp.zeros_like(acc_ref)

    acc_ref[...] += jnp.dot(a_ref[...], w_ref[...], preferred_element_type=F32)

    @pl.when(k == nk - 1)
    def _():
        y = _layer_norm(alpha * res_ref[...] + acc_ref[...], g_ref[...], b_ref[...])
        of_ref[...] = y
        ob_ref[...] = y.astype(BF16)


def proj_res_ln(a, w, res, g, b, alpha, tm=256, tk=512):
    m, kdim = a.shape
    d = w.shape[1]
    tm, tk = _fit(tm, m), _fit(tk, kdim)
    nk = kdim // tk
    return pl.pallas_call(
        functools.partial(_proj_ln_body, alpha=alpha, nk=nk),
        grid=(m // tm, nk),
        in_specs=[pl.BlockSpec((tm, tk), lambda i, k: (i, k)),
                  pl.BlockSpec((tk, d), lambda i, k: (k, 0)),
                  pl.BlockSpec((tm, d), lambda i, k: (i, 0)),
                  pl.BlockSpec((1, d), lambda i, k: (0, 0)),
                  pl.BlockSpec((1, d), lambda i, k: (0, 0))],
        out_specs=[pl.BlockSpec((tm, d), lambda i, k: (i, 0)),
                   pl.BlockSpec((tm, d), lambda i, k: (i, 0))],
        out_shape=[jax.ShapeDtypeStruct((m, d), F32), jax.ShapeDtypeStruct((m, d), BF16)],
        scratch_shapes=[pltpu.VMEM((tm, d), F32)],
        compiler_params=_params(("parallel", "arbitrary"), V7X_VMEM_LIMIT),
        name="out_proj_ln",
    )(a, w, res, g.reshape(1, d), b.reshape(1, d))


Q_SCALE = 2.0 ** math.floor(math.log2(HEAD_DIM ** -0.5))
SCORE_MUL = HEAD_DIM ** -0.5 / Q_SCALE * LOG2E


def _exp2_scores(q, k, bias2):
    t = _nt_dot(q * Q_SCALE, k) * SCORE_MUL + bias2
    return jnp.exp2(t - jnp.max(t, axis=-1, keepdims=True))


def _attend(q, k, v, bias2):
    e = _exp2_scores(q, k, bias2).astype(BF16)
    dv = v.shape[1]
    o = jnp.dot(e, jnp.concatenate([v, jnp.ones((v.shape[0], HEAD_DIM), v.dtype)], axis=-1),
                preferred_element_type=F32)
    return o[:, :dv] * (1.0 / o[:, dv:dv + 1])


def _attend_wide(q, k, v, bias2):
    e = _exp2_scores(q, k, bias2)
    o = jnp.dot(e.astype(BF16), v, preferred_element_type=F32)
    return o * (1.0 / jnp.sum(e, axis=-1, keepdims=True))


def _band_bias(band_ref, i, kb0, kb1, nq):
    return jnp.concatenate([band_ref[nq - 1 - i + kb] for kb in range(kb0, kb1)], axis=-1)


def _diff_body(dl_ref, gain_ref, q_ref, k_ref, v_ref, band_ref, o_ref, *, tq, lam_init):
    nq = k_ref.shape[0] // tq
    dl = dl_ref[...]
    lam = (jnp.exp(jnp.sum(dl[0:1] * dl[1:2], axis=-1, keepdims=True))
           - jnp.exp(jnp.sum(dl[2:3] * dl[3:4], axis=-1, keepdims=True)) + lam_init)
    for i in range(nq):
        bias2 = _band_bias(band_ref, i, 0, nq, nq)
        q = q_ref[i * tq:(i + 1) * tq, :]
        o0 = _attend_wide(q[:, :HEAD_DIM], k_ref[:, :HEAD_DIM], v_ref[...], bias2)
        o1 = _attend_wide(q[:, HEAD_DIM:], k_ref[:, HEAD_DIM:], v_ref[...], bias2)
        o = o0 - lam * o1
        o = o * lax.rsqrt(jnp.mean(o * o, axis=-1, keepdims=True) + LN_EPS) * gain_ref[...]
        o_ref[i * tq:(i + 1) * tq, :] = (o * (1.0 - lam_init)).astype(o_ref.dtype)


def _dil_body(q_ref, k_ref, v_ref, band_ref, o_ref, *, tq, reach):
    nq = k_ref.shape[0] // tq
    for i in range(nq):
        live = [kb for kb in range(nq) if kb == i or (abs(kb - i) - 1) * tq + 1 <= reach]
        kb0, kb1 = live[0], live[-1] + 1
        o = _attend(q_ref[i * tq:(i + 1) * tq, :], k_ref[kb0 * tq:kb1 * tq, :],
                    v_ref[kb0 * tq:kb1 * tq, :], _band_bias(band_ref, i, kb0, kb1, nq))
        o_ref[i * tq:(i + 1) * tq, :] = o.astype(o_ref.dtype)


def _na_body(q_ref, k_ref, v_ref, bias_ref, o_ref, *, rows_q, win):
    rows = q_ref.shape[0] // GRID_W
    nrb = rows // rows_q
    tq = rows_q * GRID_W
    for rb in range(nrb):
        w0 = min(max(rb * rows_q - NA_KH // 2, 0), rows - win) * GRID_W
        case = 0 if rb == 0 else (2 if rb == nrb - 1 else 1)
        o = _attend(q_ref[rb * tq:(rb + 1) * tq, :], k_ref[w0:w0 + win * GRID_W, :],
                    v_ref[w0:w0 + win * GRID_W, :], bias_ref[case])
        o_ref[rb * tq:(rb + 1) * tq, :] = o.astype(o_ref.dtype)


def _toeplitz(u, t):
    lead = u.shape[:-1]
    span = 2 * t - 1
    m = jnp.tile(u, (1,) * len(lead) + (t,))[..., :t * span].reshape(lead + (t, span))
    return m[..., t - 1:]


def _t5_rel_table(table, s):
    rel = jnp.arange(-(s - 1), s, dtype=jnp.int32)
    half = T5_BUCKETS // 2
    exact = half // 2
    n = jnp.abs(rel)
    log_ratio = jnp.log(jnp.maximum(n, 1).astype(F32) / exact) / math.log(T5_MAX_DIST / exact)
    large = jnp.minimum(exact + (log_ratio * (half - exact)).astype(jnp.int32), half - 1)
    bucket = jnp.where(rel > 0, half, 0) + jnp.where(n < exact, n, large)
    return table[bucket].astype(F32)


def _toeplitz_tiles(rel_tab, s, tq):
    h = rel_tab.shape[1]
    nq = s // tq
    r = jnp.concatenate([rel_tab.T, jnp.zeros((h, 1), F32)], axis=1).reshape(h, 2 * nq, tq)
    return _toeplitz(jnp.concatenate([r[:, :-1], r[:, 1:]], axis=-1), tq)


def _dilated_reach():
    return max((window // 2 // dilation) * dilation for window, dilation in DIL_BRANCHES)


def _dilated_log_multiplicity(s):
    rel = jnp.arange(-(s - 1), s, dtype=jnp.int32)
    cnt = jnp.zeros(rel.shape, F32)
    for window, dilation in DIL_BRANCHES:
        reach = (window // 2 // dilation) * dilation
        cnt = cnt + ((rel % dilation == 0) & (jnp.abs(rel) <= reach)).astype(F32)
    return jnp.where(cnt > 0, jnp.log(jnp.maximum(cnt, 1.0)), NEG)


def even_attention(h, bsz, s, d, t5_table, diff_lambda, diff_gain, lam_init, tq=256):
    n = bsz * s
    dh = HEAD_DIM
    h_diff = d // (4 * dh)
    h_dil = d // (2 * dh)
    nq = s // tq
    rel_tab = _t5_rel_table(t5_table, s)
    band_a = _toeplitz_tiles(rel_tab[:, :h_diff] * LOG2E, s, tq)
    band_b = _toeplitz_tiles((rel_tab[:, h_diff:] + _dilated_log_multiplicity(s)[:, None]) * LOG2E, s, tq)
    nb = 2 * nq - 1

    ya = pl.pallas_call(
        functools.partial(_diff_body, tq=tq, lam_init=lam_init),
        grid=(h_diff, bsz),
        in_specs=[pl.BlockSpec((4, dh), lambda hh, b: (0, 0)),
                  pl.BlockSpec((1, 2 * dh), lambda hh, b: (0, 0)),
                  pl.BlockSpec((s, 2 * dh), lambda hh, b: (b, hh)),
                  pl.BlockSpec((s, 2 * dh), lambda hh, b: (b, h_diff + hh)),
                  pl.BlockSpec((s, 2 * dh), lambda hh, b: (b, 2 * h_diff + hh)),
                  pl.BlockSpec((None, nb, tq, tq), lambda hh, b: (hh, 0, 0, 0))],
        out_specs=pl.BlockSpec((s, 2 * dh), lambda hh, b: (b, hh)),
        out_shape=jax.ShapeDtypeStruct((n, h_diff * 2 * dh), BF16),
        compiler_params=_params(("parallel", "parallel"), V7X_VMEM_LIMIT),
        name="diff_attn",
    )(diff_lambda.astype(F32), diff_gain.astype(F32).reshape(1, 2 * dh), h, h, h, band_a)

    c0 = 3 * h_diff * 2
    yb = pl.pallas_call(
        functools.partial(_dil_body, tq=tq, reach=_dilated_reach()),
        grid=(h_dil, bsz),
        in_specs=[pl.BlockSpec((s, dh), lambda hh, b: (b, c0 + hh)),
                  pl.BlockSpec((s, dh), lambda hh, b: (b, c0 + h_dil + hh)),
                  pl.BlockSpec((s, dh), lambda hh, b: (b, c0 + 2 * h_dil + hh)),
                  pl.BlockSpec((None, nb, tq, tq), lambda hh, b: (hh, 0, 0, 0))],
        out_specs=pl.BlockSpec((s, dh), lambda hh, b: (b, hh)),
        out_shape=jax.ShapeDtypeStruct((n, h_dil * dh), BF16),
        compiler_params=_params(("parallel", "parallel"), V7X_VMEM_LIMIT),
        name="dilated_attn",
    )(h, h, h, band_b)
    return jnp.concatenate([ya, yb], axis=-1)


def _na_bias_tiles(rpb, rows, rows_q, win):
    w = GRID_W
    kh = min(NA_KH, rows)
    col = jnp.arange(w, dtype=jnp.int32)
    cs = jnp.clip(col - NA_KW // 2, 0, w - NA_KW)
    col_mask = (col[None, :] >= cs[:, None]) & (col[None, :] < cs[:, None] + NA_KW)
    left = (w - 1) - (NA_KW - 1)
    u = jnp.pad(rpb.astype(F32), ((0, 0), (0, 0), (left, 2 * w - left - (2 * NA_KW - 1))),
                constant_values=NEG)
    cm = jnp.where(col_mask[None, None], _toeplitz(u, w), NEG)
    masked = jnp.full((rpb.shape[0], w, w), NEG, F32)
    nrb = rows // rows_q
    cases = []
    for rb in (0, 1, nrb - 1):
        r0 = rb * rows_q
        w0 = min(max(r0 - NA_KH // 2, 0), rows - win)
        qrows = []
        for qr in range(rows_q):
            r = r0 + qr
            rs = min(max(r - kh // 2, 0), rows - kh)
            blocks = []
            for kj in range(win):
                kr = w0 + kj
                blocks.append(cm[:, kr - r + NA_KH - 1] if rs <= kr < rs + kh else masked)
            qrows.append(jnp.concatenate(blocks, axis=-1))
        cases.append(jnp.concatenate(qrows, axis=-2))
    return jnp.stack(cases, axis=1)


def odd_attention(h, bsz, s, d, rpb, rows_q=4, win=12):
    n = bsz * s
    dh = HEAD_DIM
    nh = d // dh
    rows = s // GRID_W
    nrb = rows // rows_q
    assert rows % rows_q == 0 and nrb >= 3 and rows >= win
    assert win >= rows_q + min(NA_KH, rows) - 1 and rows_q <= NA_KH // 2
    tq = rows_q * GRID_W
    bias = _na_bias_tiles(rpb, rows, rows_q, win) * LOG2E
    return pl.pallas_call(
        functools.partial(_na_body, rows_q=rows_q, win=win),
        grid=(nh, bsz),
        in_specs=[pl.BlockSpec((s, dh), lambda hh, b: (b, hh)),
                  pl.BlockSpec((s, dh), lambda hh, b: (b, nh + hh)),
                  pl.BlockSpec((s, dh), lambda hh, b: (b, 2 * nh + hh)),
                  pl.BlockSpec((None, 3, tq, win * GRID_W), lambda hh, b: (hh, 0, 0, 0))],
        out_specs=pl.BlockSpec((s, dh), lambda hh, b: (b, hh)),
        out_shape=jax.ShapeDtypeStruct((n, d), BF16),
        compiler_params=_params(("parallel", "parallel")),
        name="na_attn",
    )(h, h, h, bias)


def _router_body(x_ref, wr_ref, rb_ref, tri_ref, eidx_ref, wts_ref, rank_ref, cnt_ref, carry_ref):
    ne, tm = wr_ref.shape[0], x_ref.shape[0]
    gsz = ne // N_GROUPS
    ninf = -jnp.inf

    @pl.when(pl.program_id(0) == 0)
    def _():
        carry_ref[...] = jnp.zeros_like(carry_ref)

    logits = _nt_dot(wr_ref[...], x_ref[...], precision=lax.Precision.HIGHEST)
    scores = jax.nn.sigmoid(logits)
    sel = scores + rb_ref[...]

    g3 = sel.reshape(N_GROUPS, gsz, tm)
    i_in = lax.broadcasted_iota(jnp.int32, g3.shape, 1)
    m1 = jnp.max(g3, axis=1, keepdims=True)
    first = jnp.min(jnp.where(g3 == m1, i_in, gsz), axis=1, keepdims=True)
    m2 = jnp.max(jnp.where(i_in == first, ninf, g3), axis=1, keepdims=True)
    grp = m1 + m2

    i_g = lax.broadcasted_iota(jnp.int32, grp.shape, 0)
    keep = jnp.zeros(grp.shape, jnp.bool_)
    for _ in range(TOPK_GROUPS):
        gm = jnp.max(grp, axis=0, keepdims=True)
        gi = jnp.min(jnp.where(grp == gm, i_g, N_GROUPS), axis=0, keepdims=True)
        hit = i_g == gi
        keep = keep | hit
        grp = jnp.where(hit, ninf, grp)
    selm = jnp.where(keep, g3, ninf).reshape(ne, tm)

    i_e = lax.broadcasted_iota(jnp.int32, (ne, tm), 0)
    hits, eids, ws = [], [], []
    for _ in range(TOP_K):
        mx = jnp.max(selm, axis=0, keepdims=True)
        ei = jnp.min(jnp.where(selm == mx, i_e, ne), axis=0, keepdims=True)
        hit = i_e == ei
        ws.append(jnp.sum(jnp.where(hit, scores, 0.0), axis=0, keepdims=True))
        selm = jnp.where(hit, ninf, selm)
        hits.append(hit)
        eids.append(ei)
    wsum = ws[0]
    for wk in ws[1:]:
        wsum = wsum + wk

    msel = jnp.zeros((ne, tm), F32)
    for hit in hits:
        msel = msel + hit.astype(F32)
    incl = jnp.dot(msel.astype(BF16), tri_ref[...], preferred_element_type=F32)
    rank_ex = incl - msel + carry_ref[:, 0:1]
    ranks = [jnp.sum(jnp.where(hit, rank_ex, 0.0), axis=0, keepdims=True) for hit in hits]

    eidx_ref[...] = jnp.concatenate(eids, axis=0)
    wts_ref[...] = jnp.concatenate([wk / wsum * ROUTED_SCALE for wk in ws], axis=0)
    rank_ref[...] = jnp.concatenate(ranks, axis=0).astype(jnp.int32)
    carry_ref[...] = carry_ref[...] + jnp.sum(msel, axis=1, keepdims=True)
    cnt_ref[...] = carry_ref[...]


def router(x, w_router, router_bias, tm=512):
    n, d = x.shape
    ne = w_router.shape[1]
    tm = _fit(tm, n)
    tri = (jnp.arange(tm)[:, None] <= jnp.arange(tm)[None, :]).astype(BF16)
    kn = jax.ShapeDtypeStruct((TOP_K, n), jnp.int32)
    eidx, wts, rank, cnt = pl.pallas_call(
        _router_body,
        grid=(n // tm,),
        in_specs=[pl.BlockSpec((tm, d), lambda t: (t, 0)),
                  pl.BlockSpec((ne, d), lambda t: (0, 0)),
                  pl.BlockSpec((ne, 1), lambda t: (0, 0)),
                  pl.BlockSpec((tm, tm), lambda t: (0, 0))],
        out_specs=[pl.BlockSpec((TOP_K, tm), lambda t: (0, t)),
                   pl.BlockSpec((TOP_K, tm), lambda t: (0, t)),
                   pl.BlockSpec((TOP_K, tm), lambda t: (0, t)),
                   pl.BlockSpec((ne, 128), lambda t: (0, 0))],
        out_shape=[kn, jax.ShapeDtypeStruct((TOP_K, n), F32), kn,
                   jax.ShapeDtypeStruct((ne, 128), F32)],
        scratch_shapes=[pltpu.VMEM((ne, 128), F32)],
        compiler_params=_params(("arbitrary",)),
        name="router",
    )(x, w_router.T.astype(F32), router_bias.astype(F32).reshape(ne, 1), tri)
    return eidx, wts, rank, cnt[:, 0].astype(jnp.int32)


def _shared_body(x_ref, xb_ref, w13_ref, w2_ref, o_ref, *, alpha):
    f = w2_ref.shape[0]
    a = jnp.dot(xb_ref[...], w13_ref[...], preferred_element_type=F32)
    hs = jax.nn.silu(a[:, :f]) * a[:, f:]
    o_ref[...] = alpha * x_ref[...] + jnp.dot(hs.astype(BF16), w2_ref[...], preferred_element_type=F32)


def shared_ffn_residual(x, xb, ws13, ws2, alpha, tm=256):
    n, d = x.shape
    f = ws2.shape[0]
    tm = _fit(tm, n)
    tok = lambda i: (i, 0)
    const = lambda i: (0, 0)
    return pl.pallas_call(
        functools.partial(_shared_body, alpha=alpha),
        grid=(n // tm,),
        in_specs=[pl.BlockSpec((tm, d), tok), pl.BlockSpec((tm, d), tok),
                  pl.BlockSpec((d, 2 * f), const), pl.BlockSpec((f, d), const)],
        out_specs=pl.BlockSpec((tm, d), tok),
        out_shape=jax.ShapeDtypeStruct((n, d), F32),
        compiler_params=_params(("parallel",), V7X_VMEM_LIMIT),
        name="shared_ffn",
    )(x, xb, ws13, ws2)


def _slab_pitch(d):
    assert d % LANES == 0
    return d // LANES + 1


def _slab_copy(src, src_row, dst, dst_row, sem, pitch):
    return pltpu.make_async_copy(src.at[pl.ds(src_row * pitch, pitch), :],
                                 dst.at[pl.ds(dst_row * pitch, pitch), :], sem)


def _slab_block(s, rows, pitch):
    return (pl.ds(s, rows, stride=pitch), slice(None))


def _dispatch_body(pos_ref, x_ref, xs_hbm, slab, sem):
    tm, d = x_ref.shape
    pitch = _slab_pitch(d)
    for s in range(pitch - 1):
        slab[_slab_block(s, tm, pitch)] = x_ref[:, s * LANES:(s + 1) * LANES]
    slab[_slab_block(pitch - 1, tm, pitch)] = jnp.zeros((tm, LANES), slab.dtype)
    for k in range(TOP_K):
        def issue(rb, c, k=k):
            for u in range(DMA_UNROLL):
                r = rb * DMA_UNROLL + u
                _slab_copy(slab, r, xs_hbm, pos_ref[k, r], sem, pitch).start()
            return c
        lax.fori_loop(0, tm // DMA_UNROLL, issue, 0)
    for k in range(TOP_K):
        pltpu.make_async_copy(slab, xs_hbm.at[pl.ds(0, tm * pitch), :], sem).wait()


def dispatch(x, pos_tiles):
    n, d = x.shape
    nt, _, tm = pos_tiles.shape
    pitch = _slab_pitch(d)
    return pl.pallas_call(
        _dispatch_body,
        grid=(nt,),
        in_specs=[pl.BlockSpec((None, TOP_K, tm), lambda i: (i, 0, 0), memory_space=pltpu.SMEM),
                  pl.BlockSpec((tm, d), lambda i: (i, 0))],
        out_specs=pl.BlockSpec(memory_space=pl.ANY),
        out_shape=jax.ShapeDtypeStruct((TOP_K * n * pitch, LANES), x.dtype),
        scratch_shapes=[pltpu.VMEM((tm * pitch, LANES), x.dtype), pltpu.SemaphoreType.DMA],
        compiler_params=_params(("arbitrary",)),
        name="dispatch",
    )(pos_tiles, x)


def _expert_body(tile_ref, exp_ref, lo_ref, hi_ref, first_ref, xs_ref, w13_ref, w2_ref, ys_ref):
    it = pl.program_id(0)
    f, d = w2_ref.shape
    pitch = _slab_pitch(d)
    tm = xs_ref.shape[0] // pitch
    per_dot = EXPERT_OUT_CHUNK // LANES
    lo, hi = lo_ref[it], hi_ref[it]

    @pl.when(hi > lo)
    def _():
        x = jnp.concatenate([xs_ref[_slab_block(s, tm, pitch)].astype(BF16) for s in range(pitch - 1)],
                            axis=-1)
        a = jnp.dot(x, w13_ref[...], preferred_element_type=F32)
        g = jax.nn.silu(a[:, :f]) * a[:, f:]
        row = tile_ref[it] * tm + lax.broadcasted_iota(jnp.int32, (tm, 1), 0)
        g = jnp.where((row >= lo) & (row < hi), g, 0.0).astype(BF16)

        def out_chunk(c):
            return jnp.dot(g, w2_ref[:, c * EXPERT_OUT_CHUNK:(c + 1) * EXPERT_OUT_CHUNK],
                           preferred_element_type=F32)

        @pl.when(first_ref[it] == 1)
        def _():
            for c in range(d // EXPERT_OUT_CHUNK):
                y = out_chunk(c)
                for j in range(per_dot):
                    ys_ref[_slab_block(c * per_dot + j, tm, pitch)] = y[:, j * LANES:(j + 1) * LANES]
            ys_ref[_slab_block(pitch - 1, tm, pitch)] = jnp.zeros((tm, LANES), F32)

        @pl.when(first_ref[it] == 0)
        def _():
            for c in range(d // EXPERT_OUT_CHUNK):
                y = out_chunk(c)
                for j in range(per_dot):
                    ys_ref[_slab_block(c * per_dot + j, tm, pitch)] += y[:, j * LANES:(j + 1) * LANES]


def expert_ffn(xs, w13, w2, items, tm):
    f, d = w2.shape[1:]
    pitch = _slab_pitch(d)
    assert d % EXPERT_OUT_CHUNK == 0 and (tm * pitch) % 8 == 0
    n_items = items[0].shape[0]
    grid_spec = pltpu.PrefetchScalarGridSpec(
        num_scalar_prefetch=5,
        grid=(n_items,),
        in_specs=[pl.BlockSpec((tm * pitch, LANES), lambda i, t, e, lo, hi, fi: (t[i], 0)),
                  pl.BlockSpec((None, d, 2 * f), lambda i, t, e, lo, hi, fi: (e[i], 0, 0)),
                  pl.BlockSpec((None, f, d), lambda i, t, e, lo, hi, fi: (e[i], 0, 0))],
        out_specs=pl.BlockSpec((tm * pitch, LANES), lambda i, t, e, lo, hi, fi: (t[i], 0)),
    )
    return pl.pallas_call(
        _expert_body,
        grid_spec=grid_spec,
        out_shape=jax.ShapeDtypeStruct(xs.shape, F32),
        compiler_params=_params(("arbitrary",), V7X_VMEM_LIMIT),
        name="expert_ffn",
    )(*items, xs, w13, w2)


def _work_items(counts, p, tm):
    ne = counts.shape[0]
    nt = p // tm
    n_items = nt + ne - 1
    ends = jnp.cumsum(counts)
    starts = ends - counts
    t_first = starts // tm
    t_last = jnp.maximum(ends - 1, 0) // tm
    per = jnp.where(counts > 0, t_last - t_first + 1, 0)
    item_end = jnp.cumsum(per)
    item_start = item_end - per
    total = item_end[-1]
    i = jnp.arange(n_items, dtype=jnp.int32)
    ic = jnp.minimum(i, total - 1)
    e = jnp.sum((item_end[None, :] <= ic[:, None]).astype(jnp.int32), axis=1)
    onehot = (e[:, None] == jnp.arange(ne)[None, :]).astype(jnp.int32)
    pick = lambda a: jnp.sum(onehot * a[None, :], axis=1)
    tile = pick(t_first) + ic - pick(item_start)
    lo = jnp.maximum(pick(starts), tile * tm)
    hi = jnp.minimum(pick(ends), (tile + 1) * tm)
    live = i < total
    lo = jnp.where(live, lo, 0)
    hi = jnp.where(live, hi, 0)
    first = (live & (lo == tile * tm)).astype(jnp.int32)
    as_i32 = lambda a: a.astype(jnp.int32)
    return as_i32(tile), as_i32(e), as_i32(lo), as_i32(hi), first


def _combine_body(pos_ref, base_ref, wt_ref, g_ref, b_ref, ys_hbm, of_ref, ob_ref, buf0, buf1, acc_ref,
                  sems):
    tm, d = base_ref.shape
    pitch = _slab_pitch(d)
    buf = (buf0, buf1)

    def issue(k, slot):
        def body(rb, c):
            for u in range(DMA_UNROLL):
                r = rb * DMA_UNROLL + u
                _slab_copy(ys_hbm, pos_ref[k, r], buf[slot], r, sems.at[slot], pitch).start()
            return c
        lax.fori_loop(0, tm // DMA_UNROLL, body, 0)

    def drain(slot):
        pltpu.make_async_copy(ys_hbm.at[pl.ds(0, tm * pitch), :], buf[slot], sems.at[slot]).wait()

    issue(0, 0)
    acc_ref[...] = base_ref[...]
    wt = wt_ref[...]
    for k in range(TOP_K):
        slot = k % 2
        if k + 1 < TOP_K:
            issue(k + 1, 1 - slot)
        drain(slot)
        wk = wt[:, k:k + 1]
        for s in range(pitch - 1):
            acc_ref[:, s * LANES:(s + 1) * LANES] += wk * buf[slot][_slab_block(s, tm, pitch)]
    y = _layer_norm(acc_ref[...], g_ref[...], b_ref[...])
    of_ref[...] = y
    ob_ref[...] = y.astype(BF16)


def combine_ln(base, ys, pos_tiles, wts_t, g, b):
    n, d = base.shape
    nt, _, tm = pos_tiles.shape
    pitch = _slab_pitch(d)
    tok = lambda i: (i, 0)
    const = lambda i: (0, 0)
    return pl.pallas_call(
        _combine_body,
        grid=(nt,),
        in_specs=[pl.BlockSpec((None, TOP_K, tm), lambda i: (i, 0, 0), memory_space=pltpu.SMEM),
                  pl.BlockSpec((tm, d), tok),
                  pl.BlockSpec((tm, TOP_K), tok),
                  pl.BlockSpec((1, d), const),
                  pl.BlockSpec((1, d), const),
                  pl.BlockSpec(memory_space=pl.ANY)],
        out_specs=[pl.BlockSpec((tm, d), tok), pl.BlockSpec((tm, d), tok)],
        out_shape=[jax.ShapeDtypeStruct((n, d), F32), jax.ShapeDtypeStruct((n, d), BF16)],
        scratch_shapes=[pltpu.VMEM((tm * pitch, LANES), F32), pltpu.VMEM((tm * pitch, LANES), F32),
                        pltpu.VMEM((tm, d), F32), pltpu.SemaphoreType.DMA((2,))],
        compiler_params=_params(("arbitrary",), V7X_VMEM_LIMIT),
        name="combine_ln",
    )(pos_tiles, base, wts_t, g.reshape(1, d), b.reshape(1, d), ys)


def moe_block(x, xb, w_router, router_bias, w1, w3, w2, ws1, ws3, ws2, g, b, alpha,
              tm_tok=256, tm_exp=256):
    n, d = x.shape
    ne = w_router.shape[1]
    tm_tok, tm_exp = _fit(tm_tok, n), _fit(tm_exp, n)
    eidx, wts, rank, counts = router(x, w_router, router_bias)
    starts = jnp.cumsum(counts) - counts
    pos = rank + jnp.sum(jnp.where(eidx[..., None] == jnp.arange(ne), starts, 0), axis=-1)
    pos_tiles = pos.reshape(TOP_K, n // tm_tok, tm_tok).transpose(1, 0, 2)
    xs = dispatch(x, pos_tiles)
    w13 = jnp.concatenate([w1.astype(BF16), w3.astype(BF16)], axis=-1)
    items = _work_items(counts, TOP_K * n, tm_exp)
    ys = expert_ffn(xs, w13, w2.astype(BF16), items, tm_exp)
    ws13 = jnp.concatenate([ws1.astype(BF16), ws3.astype(BF16)], axis=-1)
    base = shared_ffn_residual(x, xb, ws13, ws2.astype(BF16), alpha)
    return combine_ln(base, ys, pos_tiles, wts.T, g, b)


def kernel(x, t5_table, w_in_even, diff_lambda, diff_gain, w_out_even, w_in_odd, na_rpb, w_out_odd,
           ln_mix_g, ln_mix_b, router_w, router_bias, exp_w1, exp_w3, exp_w2, sh_w1, sh_w3, sh_w2,
           ln_ffn_g, ln_ffn_b):
    bsz, s, d = x.shape
    depth = ln_mix_g.shape[0]
    alpha = (2 * depth) ** 0.25
    xf = x.reshape(bsz * s, d).astype(F32)
    xb = xf.astype(BF16)
    for l in range(depth):
        j = l // 2
        if l % 2 == 0:
            h = matmul(xb, w_in_even[j].astype(BF16), BF16)
            lam_init = 0.8 - 0.6 * math.exp(-0.3 * l)
            att = even_attention(h, bsz, s, d, t5_table, diff_lambda[j], diff_gain[j], lam_init)
            w_out = w_out_even[j]
        else:
            h = matmul(xb, w_in_odd[j].astype(BF16), BF16)
            att = odd_attention(h, bsz, s, d, na_rpb[j])
            w_out = w_out_odd[j]
        xf, xb = proj_res_ln(att, w_out.astype(BF16), xf, ln_mix_g[l], ln_mix_b[l], alpha)
        xf, xb = moe_block(xf, xb, router_w[l], router_bias[l], exp_w1[l], exp_w3[l], exp_w2[l],
                           sh_w1[l], sh_w3[l], sh_w2[l], ln_ffn_g[l], ln_ffn_b[l], alpha)
    return xf.reshape(bsz, s, d).astype(x.dtype)
```

```python
import functools
import math

import jax
import jax.numpy as jnp
from jax import lax
from jax.experimental import pallas as pl
from jax.experimental.pallas import tpu as pltpu

HEAD_DIM = 128
DIL_BRANCHES = ((128, 1), (512, 4), (2048, 16))
GRID_W = 64
NA_KH = 8
NA_KW = 16
T5_BUCKETS = 32
T5_MAX_DIST = 1024
N_GROUPS = 8
TOPK_GROUPS = 4
TOP_K = 8
ROUTED_SCALE = 2.5
LN_EPS = 1e-5
NEG = -1e30
LOG2E = math.log2(math.e)

V7X_VMEM_LIMIT = 56 * 1024 * 1024
LANES = 128
EXPERT_OUT_CHUNK = 512
DMA_UNROLL = 8
COMBINE_ROWS = 32

F32 = jnp.float32
BF16 = jnp.bfloat16
U32 = jnp.uint32
HIGH_HALF = 0xFFFF0000


def _params(sem, vmem=None):
    return pltpu.CompilerParams(dimension_semantics=sem, vmem_limit_bytes=vmem)


def _fit(tile, dim):
    if dim <= tile:
        return dim
    return max(t for t in range(128, tile + 1, 128) if dim % t == 0)


def _nt_dot(a, b, **kw):
    return lax.dot_general(a, b, (((1,), (1,)), ((), ())), preferred_element_type=F32, **kw)


def _layer_norm(z, g, b):
    mu = jnp.mean(z, axis=-1, keepdims=True)
    zc = z - mu
    var = jnp.mean(zc * zc, axis=-1, keepdims=True)
    return zc * lax.rsqrt(var + LN_EPS) * g + b


def _mm_body(x_ref, w_ref, o_ref):
    o_ref[...] = jnp.dot(x_ref[...], w_ref[...], preferred_element_type=F32).astype(o_ref.dtype)


def matmul(x, w, out_dtype, tm=512, tn=1024):
    m, k = x.shape
    n = w.shape[1]
    tm, tn = _fit(tm, m), _fit(tn, n)
    return pl.pallas_call(
        _mm_body,
        grid=(n // tn, m // tm),
        in_specs=[pl.BlockSpec((tm, k), lambda j, i: (i, 0)),
                  pl.BlockSpec((k, tn), lambda j, i: (0, j))],
        out_specs=pl.BlockSpec((tm, tn), lambda j, i: (i, j)),
        out_shape=jax.ShapeDtypeStruct((m, n), out_dtype),
        compiler_params=_params(("parallel", "parallel"), V7X_VMEM_LIMIT),
        name="in_proj",
    )(x, w)


def _proj_ln_body(a_ref, w_ref, res_ref, g_ref, b_ref, of_ref, ob_ref, *, alpha, nk, ln_rows):
    k = pl.program_id(1)

    @pl.when(k == 0)
    def _():
        of_ref[...] = jnp.dot(a_ref[...], w_ref[...], preferred_element_type=F32)

    @pl.when(k > 0)
    def _():
        of_ref[...] += jnp.dot(a_ref[...], w_ref[...], preferred_element_type=F32)

    @pl.when(k == nk - 1)
    def _():
        def chunk(c, carry):
            rows = pl.ds(pl.multiple_of(c * ln_rows, ln_rows), ln_rows)
            y = _layer_norm(alpha * res_ref[rows, :] + of_ref[rows, :], g_ref[...], b_ref[...])
            of_ref[rows, :] = y
            ob_ref[rows, :] = y.astype(BF16)
            return carry
        lax.fori_loop(0, of_ref.shape[0] // ln_rows, chunk, 0)


def proj_res_ln(a, w, res, g, b, alpha, tm=512, tk=256):
    m, kdim = a.shape
    d = w.shape[1]
    tm, tk = _fit(tm, m), _fit(tk, kdim)
    nk = kdim // tk
    return pl.pallas_call(
        functools.partial(_proj_ln_body, alpha=alpha, nk=nk, ln_rows=min(64, tm)),
        grid=(m // tm, nk),
        in_specs=[pl.BlockSpec((tm, tk), lambda i, k: (i, k)),
                  pl.BlockSpec((tk, d), lambda i, k: (k, 0)),
                  pl.BlockSpec((tm, d), lambda i, k: (i, 0)),
                  pl.BlockSpec((1, d), lambda i, k: (0, 0)),
                  pl.BlockSpec((1, d), lambda i, k: (0, 0))],
        out_specs=[pl.BlockSpec((tm, d), lambda i, k: (i, 0)),
                   pl.BlockSpec((tm, d), lambda i, k: (i, 0))],
        out_shape=[jax.ShapeDtypeStruct((m, d), F32), jax.ShapeDtypeStruct((m, d), BF16)],
        compiler_params=_params(("parallel", "arbitrary"), V7X_VMEM_LIMIT),
        name="out_proj_ln",
    )(a, w, res, g.reshape(1, d), b.reshape(1, d))


Q_SCALE = 2.0 ** math.floor(math.log2(HEAD_DIM ** -0.5))
SCORE_MUL = HEAD_DIM ** -0.5 / Q_SCALE * LOG2E


def _exp2_scores(q, k, bias2):
    t = _nt_dot(q * Q_SCALE, k) * SCORE_MUL + bias2
    return jnp.exp2(t - jnp.max(t, axis=-1, keepdims=True))


def _attend(q, k, v, bias2):
    e = _exp2_scores(q, k, bias2).astype(BF16)
    dv = v.shape[1]
    o = jnp.dot(e, jnp.concatenate([v, jnp.ones((v.shape[0], HEAD_DIM), v.dtype)], axis=-1),
                preferred_element_type=F32)
    return o[:, :dv] * (1.0 / o[:, dv:dv + 1])


def _attend_wide(q, k, v, bias2):
    e = _exp2_scores(q, k, bias2)
    o = jnp.dot(e.astype(BF16), v, preferred_element_type=F32)
    return o * (1.0 / jnp.sum(e, axis=-1, keepdims=True))


def _band_bias(band_ref, i, kb0, kb1, nq):
    return jnp.concatenate([band_ref[nq - 1 - i + kb] for kb in range(kb0, kb1)], axis=-1)


def _diff_body(dl_ref, gain_ref, q_ref, k_ref, v_ref, band_ref, o_ref, *, tq, lam_init):
    nq = k_ref.shape[0] // tq
    dl = dl_ref[...]
    lam = (jnp.exp(jnp.sum(dl[0:1] * dl[1:2], axis=-1, keepdims=True))
           - jnp.exp(jnp.sum(dl[2:3] * dl[3:4], axis=-1, keepdims=True)) + lam_init)
    for i in range(nq):
        bias2 = _band_bias(band_ref, i, 0, nq, nq)
        q = q_ref[i * tq:(i + 1) * tq, :]
        o0 = _attend_wide(q[:, :HEAD_DIM], k_ref[:, :HEAD_DIM], v_ref[...], bias2)
        o1 = _attend_wide(q[:, HEAD_DIM:], k_ref[:, HEAD_DIM:], v_ref[...], bias2)
        o = o0 - lam * o1
        o = o * lax.rsqrt(jnp.mean(o * o, axis=-1, keepdims=True) + LN_EPS) * gain_ref[...]
        o_ref[i * tq:(i + 1) * tq, :] = (o * (1.0 - lam_init)).astype(o_ref.dtype)


def _dil_body(q_ref, k_ref, v_ref, band_ref, o_ref, *, tq, reach):
    nq = k_ref.shape[0] // tq
    for i in range(nq):
        live = [kb for kb in range(nq) if kb == i or (abs(kb - i) - 1) * tq + 1 <= reach]
        kb0, kb1 = live[0], live[-1] + 1
        o = _attend(q_ref[i * tq:(i + 1) * tq, :], k_ref[kb0 * tq:kb1 * tq, :],
                    v_ref[kb0 * tq:kb1 * tq, :], _band_bias(band_ref, i, kb0, kb1, nq))
        o_ref[i * tq:(i + 1) * tq, :] = o.astype(o_ref.dtype)


def _na_body(q_ref, k_ref, v_ref, bias_ref, o_ref, *, rows_q, win):
    rows = q_ref.shape[0] // GRID_W
    nrb = rows // rows_q
    tq = rows_q * GRID_W
    for rb in range(nrb):
        w0 = min(max(rb * rows_q - NA_KH // 2, 0), rows - win) * GRID_W
        case = 0 if rb == 0 else (2 if rb == nrb - 1 else 1)
        o = _attend(q_ref[rb * tq:(rb + 1) * tq, :], k_ref[w0:w0 + win * GRID_W, :],
                    v_ref[w0:w0 + win * GRID_W, :], bias_ref[case])
        o_ref[rb * tq:(rb + 1) * tq, :] = o.astype(o_ref.dtype)


def _toeplitz(u, t):
    lead = u.shape[:-1]
    span = 2 * t - 1
    m = jnp.tile(u, (1,) * len(lead) + (t,))[..., :t * span].reshape(lead + (t, span))
    return m[..., t - 1:]


def _t5_rel_table(table, s):
    rel = jnp.arange(-(s - 1), s, dtype=jnp.int32)
    half = T5_BUCKETS // 2
    exact = half // 2
    n = jnp.abs(rel)
    log_ratio = jnp.log(jnp.maximum(n, 1).astype(F32) / exact) / math.log(T5_MAX_DIST / exact)
    large = jnp.minimum(exact + (log_ratio * (half - exact)).astype(jnp.int32), half - 1)
    bucket = jnp.where(rel > 0, half, 0) + jnp.where(n < exact, n, large)
    return table[bucket].astype(F32)


def _toeplitz_tiles(rel_tab, s, tq):
    h = rel_tab.shape[1]
    nq = s // tq
    r = jnp.concatenate([rel_tab.T, jnp.zeros((h, 1), F32)], axis=1).reshape(h, 2 * nq, tq)
    return _toeplitz(jnp.concatenate([r[:, :-1], r[:, 1:]], axis=-1), tq)


def _dilated_reach():
    return max((window // 2 // dilation) * dilation for window, dilation in DIL_BRANCHES)


def _dilated_log_multiplicity(s):
    rel = jnp.arange(-(s - 1), s, dtype=jnp.int32)
    cnt = jnp.zeros(rel.shape, F32)
    for window, dilation in DIL_BRANCHES:
        reach = (window // 2 // dilation) * dilation
        cnt = cnt + ((rel % dilation == 0) & (jnp.abs(rel) <= reach)).astype(F32)
    return jnp.where(cnt > 0, jnp.log(jnp.maximum(cnt, 1.0)), NEG)


def even_attention(h, bsz, s, d, t5_table, diff_lambda, diff_gain, lam_init, tq=256):
    n = bsz * s
    dh = HEAD_DIM
    h_diff = d // (4 * dh)
    h_dil = d // (2 * dh)
    nq = s // tq
    rel_tab = _t5_rel_table(t5_table, s)
    band_a = _toeplitz_tiles(rel_tab[:, :h_diff] * LOG2E, s, tq)
    band_b = _toeplitz_tiles((rel_tab[:, h_diff:] + _dilated_log_multiplicity(s)[:, None]) * LOG2E, s, tq)
    nb = 2 * nq - 1

    ya = pl.pallas_call(
        functools.partial(_diff_body, tq=tq, lam_init=lam_init),
        grid=(h_diff, bsz),
        in_specs=[pl.BlockSpec((4, dh), lambda hh, b: (0, 0)),
                  pl.BlockSpec((1, 2 * dh), lambda hh, b: (0, 0)),
                  pl.BlockSpec((s, 2 * dh), lambda hh, b: (b, hh)),
                  pl.BlockSpec((s, 2 * dh), lambda hh, b: (b, h_diff + hh)),
                  pl.BlockSpec((s, 2 * dh), lambda hh, b: (b, 2 * h_diff + hh)),
                  pl.BlockSpec((None, nb, tq, tq), lambda hh, b: (hh, 0, 0, 0))],
        out_specs=pl.BlockSpec((s, 2 * dh), lambda hh, b: (b, hh)),
        out_shape=jax.ShapeDtypeStruct((n, h_diff * 2 * dh), BF16),
        compiler_params=_params(("parallel", "parallel"), V7X_VMEM_LIMIT),
        name="diff_attn",
    )(diff_lambda.astype(F32), diff_gain.astype(F32).reshape(1, 2 * dh), h, h, h, band_a)

    c0 = 3 * h_diff * 2
    yb = pl.pallas_call(
        functools.partial(_dil_body, tq=tq, reach=_dilated_reach()),
        grid=(h_dil, bsz),
        in_specs=[pl.BlockSpec((s, dh), lambda hh, b: (b, c0 + hh)),
                  pl.BlockSpec((s, dh), lambda hh, b: (b, c0 + h_dil + hh)),
                  pl.BlockSpec((s, dh), lambda hh, b: (b, c0 + 2 * h_dil + hh)),
                  pl.BlockSpec((None, nb, tq, tq), lambda hh, b: (hh, 0, 0, 0))],
        out_specs=pl.BlockSpec((s, dh), lambda hh, b: (b, hh)),
        out_shape=jax.ShapeDtypeStruct((n, h_dil * dh), BF16),
        compiler_params=_params(("parallel", "parallel"), V7X_VMEM_LIMIT),
        name="dilated_attn",
    )(h, h, h, band_b)
    return jnp.concatenate([ya, yb], axis=-1)


def _na_bias_tiles(rpb, rows, rows_q, win):
    w = GRID_W
    kh = min(NA_KH, rows)
    col = jnp.arange(w, dtype=jnp.int32)
    cs = jnp.clip(col - NA_KW // 2, 0, w - NA_KW)
    col_mask = (col[None, :] >= cs[:, None]) & (col[None, :] < cs[:, None] + NA_KW)
    left = (w - 1) - (NA_KW - 1)
    u = jnp.pad(rpb.astype(F32), ((0, 0), (0, 0), (left, 2 * w - left - (2 * NA_KW - 1))),
                constant_values=NEG)
    cm = jnp.where(col_mask[None, None], _toeplitz(u, w), NEG)
    masked = jnp.full((rpb.shape[0], w, w), NEG, F32)
    nrb = rows // rows_q
    cases = []
    for rb in (0, 1, nrb - 1):
        r0 = rb * rows_q
        w0 = min(max(r0 - NA_KH // 2, 0), rows - win)
        qrows = []
        for qr in range(rows_q):
            r = r0 + qr
            rs = min(max(r - kh // 2, 0), rows - kh)
            blocks = []
            for kj in range(win):
                kr = w0 + kj
                blocks.append(cm[:, kr - r + NA_KH - 1] if rs <= kr < rs + kh else masked)
            qrows.append(jnp.concatenate(blocks, axis=-1))
        cases.append(jnp.concatenate(qrows, axis=-2))
    return jnp.stack(cases, axis=1)


def odd_attention(h, bsz, s, d, rpb, rows_q=4, win=12):
    n = bsz * s
    dh = HEAD_DIM
    nh = d // dh
    rows = s // GRID_W
    nrb = rows // rows_q
    assert rows % rows_q == 0 and nrb >= 3 and rows >= win
    assert win >= rows_q + min(NA_KH, rows) - 1 and rows_q <= NA_KH // 2
    tq = rows_q * GRID_W
    bias = _na_bias_tiles(rpb, rows, rows_q, win) * LOG2E
    return pl.pallas_call(
        functools.partial(_na_body, rows_q=rows_q, win=win),
        grid=(nh, bsz),
        in_specs=[pl.BlockSpec((s, dh), lambda hh, b: (b, hh)),
                  pl.BlockSpec((s, dh), lambda hh, b: (b, nh + hh)),
                  pl.BlockSpec((s, dh), lambda hh, b: (b, 2 * nh + hh)),
                  pl.BlockSpec((None, 3, tq, win * GRID_W), lambda hh, b: (hh, 0, 0, 0))],
        out_specs=pl.BlockSpec((s, dh), lambda hh, b: (b, hh)),
        out_shape=jax.ShapeDtypeStruct((n, d), BF16),
        compiler_params=_params(("parallel", "parallel")),
        name="na_attn",
    )(h, h, h, bias)


def _router_body(x_ref, wr_ref, rb_ref, tri_ref, eidx_ref, wts_ref, rank_ref, cnt_ref, carry_ref):
    ne, tm = wr_ref.shape[0], x_ref.shape[0]
    gsz = ne // N_GROUPS
    ninf = -jnp.inf

    @pl.when(pl.program_id(0) == 0)
    def _():
        carry_ref[...] = jnp.zeros_like(carry_ref)

    logits = _nt_dot(wr_ref[...], x_ref[...], precision=lax.Precision.HIGHEST)
    scores = jax.nn.sigmoid(logits)
    sel = scores + rb_ref[...]

    g3 = sel.reshape(N_GROUPS, gsz, tm)
    i_in = lax.broadcasted_iota(jnp.int32, g3.shape, 1)
    m1 = jnp.max(g3, axis=1, keepdims=True)
    first = jnp.min(jnp.where(g3 == m1, i_in, gsz), axis=1, keepdims=True)
    m2 = jnp.max(jnp.where(i_in == first, ninf, g3), axis=1, keepdims=True)
    grp = m1 + m2

    i_g = lax.broadcasted_iota(jnp.int32, grp.shape, 0)
    keep = jnp.zeros(grp.shape, jnp.bool_)
    for _ in range(TOPK_GROUPS):
        gm = jnp.max(grp, axis=0, keepdims=True)
        gi = jnp.min(jnp.where(grp == gm, i_g, N_GROUPS), axis=0, keepdims=True)
        hit = i_g == gi
        keep = keep | hit
        grp = jnp.where(hit, ninf, grp)
    selm = jnp.where(keep, g3, ninf).reshape(ne, tm)

    i_e = lax.broadcasted_iota(jnp.int32, (ne, tm), 0)
    hits, eids, ws = [], [], []
    for _ in range(TOP_K):
        mx = jnp.max(selm, axis=0, keepdims=True)
        ei = jnp.min(jnp.where(selm == mx, i_e, ne), axis=0, keepdims=True)
        hit = i_e == ei
        ws.append(jnp.sum(jnp.where(hit, scores, 0.0), axis=0, keepdims=True))
        selm = jnp.where(hit, ninf, selm)
        hits.append(hit)
        eids.append(ei)
    wsum = ws[0]
    for wk in ws[1:]:
        wsum = wsum + wk

    msel = jnp.zeros((ne, tm), F32)
    for hit in hits:
        msel = msel + hit.astype(F32)
    incl = jnp.dot(msel.astype(BF16), tri_ref[...], preferred_element_type=F32)
    rank_ex = incl - msel + carry_ref[:, 0:1]
    ranks = [jnp.sum(jnp.where(hit, rank_ex, 0.0), axis=0, keepdims=True) for hit in hits]

    eidx_ref[...] = jnp.concatenate(eids, axis=0)
    wts_ref[...] = jnp.concatenate([wk / wsum * ROUTED_SCALE for wk in ws], axis=0)
    rank_ref[...] = jnp.concatenate(ranks, axis=0).astype(jnp.int32)
    carry_ref[...] = carry_ref[...] + jnp.sum(msel, axis=1, keepdims=True)
    cnt_ref[...] = carry_ref[...]


def router(x, w_router, router_bias, tm=512):
    n, d = x.shape
    ne = w_router.shape[1]
    tm = _fit(tm, n)
    tri = (jnp.arange(tm)[:, None] <= jnp.arange(tm)[None, :]).astype(BF16)
    kn = jax.ShapeDtypeStruct((TOP_K, n), jnp.int32)
    eidx, wts, rank, cnt = pl.pallas_call(
        _router_body,
        grid=(n // tm,),
        in_specs=[pl.BlockSpec((tm, d), lambda t: (t, 0)),
                  pl.BlockSpec((ne, d), lambda t: (0, 0)),
                  pl.BlockSpec((ne, 1), lambda t: (0, 0)),
                  pl.BlockSpec((tm, tm), lambda t: (0, 0))],
        out_specs=[pl.BlockSpec((TOP_K, tm), lambda t: (0, t)),
                   pl.BlockSpec((TOP_K, tm), lambda t: (0, t)),
                   pl.BlockSpec((TOP_K, tm), lambda t: (0, t)),
                   pl.BlockSpec((ne, 128), lambda t: (0, 0))],
        out_shape=[kn, jax.ShapeDtypeStruct((TOP_K, n), F32), kn,
                   jax.ShapeDtypeStruct((ne, 128), F32)],
        scratch_shapes=[pltpu.VMEM((ne, 128), F32)],
        compiler_params=_params(("arbitrary",)),
        name="router",
    )(x, w_router.T.astype(F32), router_bias.astype(F32).reshape(ne, 1), tri)
    return eidx, wts, rank, cnt[:, 0].astype(jnp.int32)


def _shared_body(x_ref, xb_ref, w13_ref, w2_ref, o_ref, *, alpha):
    f = w2_ref.shape[0]
    a = jnp.dot(xb_ref[...], w13_ref[...], preferred_element_type=F32)
    hs = jax.nn.silu(a[:, :f]) * a[:, f:]
    o_ref[...] = alpha * x_ref[...] + jnp.dot(hs.astype(BF16), w2_ref[...], preferred_element_type=F32)


def shared_ffn_residual(x, xb, ws13, ws2, alpha, tm=256):
    n, d = x.shape
    f = ws2.shape[0]
    tm = _fit(tm, n)
    tok = lambda i: (i, 0)
    const = lambda i: (0, 0)
    return pl.pallas_call(
        functools.partial(_shared_body, alpha=alpha),
        grid=(n // tm,),
        in_specs=[pl.BlockSpec((tm, d), tok), pl.BlockSpec((tm, d), tok),
                  pl.BlockSpec((d, 2 * f), const), pl.BlockSpec((f, d), const)],
        out_specs=pl.BlockSpec((tm, d), tok),
        out_shape=jax.ShapeDtypeStruct((n, d), F32),
        compiler_params=_params(("parallel",), V7X_VMEM_LIMIT),
        name="shared_ffn",
    )(x, xb, ws13, ws2)


def _slab_pitch(d):
    assert d % (2 * LANES) == 0
    return d // (2 * LANES) + 1


def _pack_bf16_pair(lo, hi):
    lo_bits = lax.bitcast_convert_type(lo.astype(BF16).astype(F32), U32)
    hi_bits = lax.bitcast_convert_type(hi.astype(BF16).astype(F32), U32)
    return (lo_bits >> 16) | (hi_bits & U32(HIGH_HALF))


def _unpack_bf16_pair(words):
    lo = lax.bitcast_convert_type(words << 16, F32)
    hi = lax.bitcast_convert_type(words & U32(HIGH_HALF), F32)
    return lo, hi


def _slab_copy(src, src_row, dst, dst_row, sem, pitch):
    return pltpu.make_async_copy(src.at[pl.ds(src_row * pitch, pitch), :],
                                 dst.at[pl.ds(dst_row * pitch, pitch), :], sem)


def _slab_block(s, rows, pitch):
    return (pl.ds(s, rows, stride=pitch), slice(None))


def _dispatch_body(pos_ref, x_ref, xs_hbm, slab, sem):
    tm, d = x_ref.shape
    pitch = _slab_pitch(d)
    half = d // 2
    for s in range(pitch - 1):
        slab[_slab_block(s, tm, pitch)] = _pack_bf16_pair(
            x_ref[:, s * LANES:(s + 1) * LANES], x_ref[:, half + s * LANES:half + (s + 1) * LANES])
    slab[_slab_block(pitch - 1, tm, pitch)] = jnp.zeros((tm, LANES), slab.dtype)
    for k in range(TOP_K):
        def issue(rb, c, k=k):
            for u in range(DMA_UNROLL):
                r = rb * DMA_UNROLL + u
                _slab_copy(slab, r, xs_hbm, pos_ref[k, r], sem, pitch).start()
            return c
        lax.fori_loop(0, tm // DMA_UNROLL, issue, 0)
    for k in range(TOP_K):
        pltpu.make_async_copy(slab, xs_hbm.at[pl.ds(0, tm * pitch), :], sem).wait()


def dispatch(x, pos_tiles):
    n, d = x.shape
    nt, _, tm = pos_tiles.shape
    pitch = _slab_pitch(d)
    return pl.pallas_call(
        _dispatch_body,
        grid=(nt,),
        in_specs=[pl.BlockSpec((None, TOP_K, tm), lambda i: (i, 0, 0), memory_space=pltpu.SMEM),
                  pl.BlockSpec((tm, d), lambda i: (i, 0))],
        out_specs=pl.BlockSpec(memory_space=pl.ANY),
        out_shape=jax.ShapeDtypeStruct((TOP_K * n * pitch, LANES), U32),
        scratch_shapes=[pltpu.VMEM((tm * pitch, LANES), U32), pltpu.SemaphoreType.DMA],
        compiler_params=_params(("arbitrary",)),
        name="dispatch",
    )(pos_tiles, x)


def _expert_body(tile_ref, exp_ref, lo_ref, hi_ref, first_ref, xs_ref, w13_ref, w2_ref, ys_ref):
    it = pl.program_id(0)
    f, d = w2_ref.shape
    pitch = _slab_pitch(d)
    half = d // 2
    tm = xs_ref.shape[0] // pitch
    per_dot = EXPERT_OUT_CHUNK // LANES
    lo, hi = lo_ref[it], hi_ref[it]

    @pl.when(hi > lo)
    def _():
        pairs = [_unpack_bf16_pair(xs_ref[_slab_block(s, tm, pitch)]) for s in range(pitch - 1)]
        x = jnp.concatenate([p[0].astype(BF16) for p in pairs] + [p[1].astype(BF16) for p in pairs],
                            axis=-1)
        a = jnp.dot(x, w13_ref[...], preferred_element_type=F32)
        g = jax.nn.silu(a[:, :f]) * a[:, f:]
        row = tile_ref[it] * tm + lax.broadcasted_iota(jnp.int32, (tm, 1), 0)
        mine = (row >= lo) & (row < hi)
        g = jnp.where(mine, g, 0.0).astype(BF16)

        def out_words(c):
            c0 = c * EXPERT_OUT_CHUNK
            y_lo = jnp.dot(g, w2_ref[:, c0:c0 + EXPERT_OUT_CHUNK], preferred_element_type=F32)
            y_hi = jnp.dot(g, w2_ref[:, half + c0:half + c0 + EXPERT_OUT_CHUNK],
                           preferred_element_type=F32)
            return _pack_bf16_pair(y_lo, y_hi)

        @pl.when(first_ref[it] == 1)
        def _():
            for c in range(half // EXPERT_OUT_CHUNK):
                w = out_words(c)
                for j in range(per_dot):
                    ys_ref[_slab_block(c * per_dot + j, tm, pitch)] = w[:, j * LANES:(j + 1) * LANES]
            ys_ref[_slab_block(pitch - 1, tm, pitch)] = jnp.zeros((tm, LANES), U32)

        @pl.when(first_ref[it] == 0)
        def _():
            for c in range(half // EXPERT_OUT_CHUNK):
                w = out_words(c)
                for j in range(per_dot):
                    blk = _slab_block(c * per_dot + j, tm, pitch)
                    ys_ref[blk] = jnp.where(mine, w[:, j * LANES:(j + 1) * LANES], ys_ref[blk])


def expert_ffn(xs, w13, w2, items, tm):
    f, d = w2.shape[1:]
    pitch = _slab_pitch(d)
    assert (d // 2) % EXPERT_OUT_CHUNK == 0 and (tm * pitch) % 8 == 0
    n_items = items[0].shape[0]
    grid_spec = pltpu.PrefetchScalarGridSpec(
        num_scalar_prefetch=5,
        grid=(n_items,),
        in_specs=[pl.BlockSpec((tm * pitch, LANES), lambda i, t, e, lo, hi, fi: (t[i], 0)),
                  pl.BlockSpec((None, d, 2 * f), lambda i, t, e, lo, hi, fi: (e[i], 0, 0)),
                  pl.BlockSpec((None, f, d), lambda i, t, e, lo, hi, fi: (e[i], 0, 0))],
        out_specs=pl.BlockSpec((tm * pitch, LANES), lambda i, t, e, lo, hi, fi: (t[i], 0)),
    )
    return pl.pallas_call(
        _expert_body,
        grid_spec=grid_spec,
        out_shape=jax.ShapeDtypeStruct(xs.shape, U32),
        compiler_params=_params(("arbitrary",), V7X_VMEM_LIMIT),
        name="expert_ffn",
    )(*items, xs, w13, w2)


def _work_items(counts, p, tm):
    ne = counts.shape[0]
    nt = p // tm
    n_items = nt + ne - 1
    ends = jnp.cumsum(counts)
    starts = ends - counts
    t_first = starts // tm
    t_last = jnp.maximum(ends - 1, 0) // tm
    per = jnp.where(counts > 0, t_last - t_first + 1, 0)
    item_end = jnp.cumsum(per)
    item_start = item_end - per
    total = item_end[-1]
    i = jnp.arange(n_items, dtype=jnp.int32)
    ic = jnp.minimum(i, total - 1)
    e = jnp.sum((item_end[None, :] <= ic[:, None]).astype(jnp.int32), axis=1)
    onehot = (e[:, None] == jnp.arange(ne)[None, :]).astype(jnp.int32)
    pick = lambda a: jnp.sum(onehot * a[None, :], axis=1)
    tile = pick(t_first) + ic - pick(item_start)
    lo = jnp.maximum(pick(starts), tile * tm)
    hi = jnp.minimum(pick(ends), (tile + 1) * tm)
    live = i < total
    lo = jnp.where(live, lo, 0)
    hi = jnp.where(live, hi, 0)
    first = (live & (lo == tile * tm)).astype(jnp.int32)
    as_i32 = lambda a: a.astype(jnp.int32)
    return as_i32(tile), as_i32(e), as_i32(lo), as_i32(hi), first


def _combine_body(pos_ref, base_ref, wt_ref, g_ref, b_ref, ys_hbm, of_ref, ob_ref, *scratch):
    bufs, acc_ref, sem = scratch[:TOP_K], scratch[TOP_K], scratch[TOP_K + 1]
    tm, d = base_ref.shape
    pitch = _slab_pitch(d)
    half = d // 2

    for k in range(TOP_K):
        def issue(rb, c, k=k):
            for u in range(DMA_UNROLL):
                r = rb * DMA_UNROLL + u
                _slab_copy(ys_hbm, pos_ref[k, r], bufs[k], r, sem, pitch).start()
            return c
        lax.fori_loop(0, tm // DMA_UNROLL, issue, 0)
    for k in range(TOP_K):
        pltpu.make_async_copy(ys_hbm.at[pl.ds(0, tm * pitch), :], bufs[k], sem).wait()

    for r0 in range(0, tm, COMBINE_ROWS):
        rows = slice(r0, r0 + COMBINE_ROWS)
        gates = [jnp.broadcast_to(wt_ref[rows, k:k + 1], (COMBINE_ROWS, LANES)) for k in range(TOP_K)]
        for s in range(pitch - 1):
            lo_cols = slice(s * LANES, (s + 1) * LANES)
            hi_cols = slice(half + s * LANES, half + (s + 1) * LANES)
            lo_acc = base_ref[rows, lo_cols]
            hi_acc = base_ref[rows, hi_cols]
            for k in range(TOP_K):
                lo, hi = _unpack_bf16_pair(bufs[k][pl.ds(r0 * pitch + s, COMBINE_ROWS, stride=pitch), :])
                lo_acc = lo_acc + gates[k] * lo
                hi_acc = hi_acc + gates[k] * hi
            acc_ref[rows, lo_cols] = lo_acc
            acc_ref[rows, hi_cols] = hi_acc
    y = _layer_norm(acc_ref[...], g_ref[...], b_ref[...])
    of_ref[...] = y
    ob_ref[...] = y.astype(BF16)


def combine_ln(base, ys, pos_tiles, wts_t, g, b):
    n, d = base.shape
    nt, _, tm = pos_tiles.shape
    pitch = _slab_pitch(d)
    tok = lambda i: (i, 0)
    const = lambda i: (0, 0)
    return pl.pallas_call(
        _combine_body,
        grid=(nt,),
        in_specs=[pl.BlockSpec((None, TOP_K, tm), lambda i: (i, 0, 0), memory_space=pltpu.SMEM),
                  pl.BlockSpec((tm, d), tok),
                  pl.BlockSpec((tm, TOP_K), tok),
                  pl.BlockSpec((1, d), const),
                  pl.BlockSpec((1, d), const),
                  pl.BlockSpec(memory_space=pl.ANY)],
        out_specs=[pl.BlockSpec((tm, d), tok), pl.BlockSpec((tm, d), tok)],
        out_shape=[jax.ShapeDtypeStruct((n, d), F32), jax.ShapeDtypeStruct((n, d), BF16)],
        scratch_shapes=[pltpu.VMEM((tm * pitch, LANES), U32) for _ in range(TOP_K)]
        + [pltpu.VMEM((tm, d), F32), pltpu.SemaphoreType.DMA],
        compiler_params=_params(("arbitrary",), V7X_VMEM_LIMIT),
        name="combine_ln",
    )(pos_tiles, base, wts_t, g.reshape(1, d), b.reshape(1, d), ys)


def moe_block(x, xb, w_router, router_bias, w1, w3, w2, ws1, ws3, ws2, g, b, alpha,
              tm_tok=256, tm_exp=256):
    n, d = x.shape
    ne = w_router.shape[1]
    tm_tok, tm_exp = _fit(tm_tok, n), _fit(tm_exp, n)
    eidx, wts, rank, counts = router(x, w_router, router_bias)
    starts = jnp.cumsum(counts) - counts
    pos = rank + jnp.sum(jnp.where(eidx[..., None] == jnp.arange(ne), starts, 0), axis=-1)
    pos_tiles = pos.reshape(TOP_K, n // tm_tok, tm_tok).transpose(1, 0, 2)
    xs = dispatch(x, pos_tiles)
    w13 = jnp.concatenate([w1.astype(BF16), w3.astype(BF16)], axis=-1)
    items = _work_items(counts, TOP_K * n, tm_exp)
    ys = expert_ffn(xs, w13, w2.astype(BF16), items, tm_exp)
    ws13 = jnp.concatenate([ws1.astype(BF16), ws3.astype(BF16)], axis=-1)
    base = shared_ffn_residual(x, xb, ws13, ws2.astype(BF16), alpha)
    return combine_ln(base, ys, pos_tiles, wts.T, g, b)


def kernel(x, t5_table, w_in_even, diff_lambda, diff_gain, w_out_even, w_in_odd, na_rpb, w_out_odd,
           ln_mix_g, ln_mix_b, router_w, router_bias, exp_w1, exp_w3, exp_w2, sh_w1, sh_w3, sh_w2,
           ln_ffn_g, ln_ffn_b):
    bsz, s, d = x.shape
    depth = ln_mix_g.shape[0]
    alpha = (2 * depth) ** 0.25
    xf = x.reshape(bsz * s, d).astype(F32)
    xb = xf.astype(BF16)
    for l in range(depth):
        j = l // 2
        if l % 2 == 0:
            h = matmul(xb, w_in_even[j].astype(BF16), BF16)
            lam_init = 0.8 - 0.6 * math.exp(-0.3 * l)
            att = even_attention(h, bsz, s, d, t5_table, diff_lambda[j], diff_gain[j], lam_init)
            w_out = w_out_even[j]
        else:
            h = matmul(xb, w_in_odd[j].astype(BF16), BF16)
            att = odd_attention(h, bsz, s, d, na_rpb[j])
            w_out = w_out_odd[j]
        xf, xb = proj_res_ln(att, w_out.astype(BF16), xf, ln_mix_g[l], ln_mix_b[l], alpha)
        xf, xb = moe_block(xf, xb, router_w[l], router_bias[l], exp_w1[l], exp_w3[l], exp_w2[l],
                           sh_w1[l], sh_w3[l], sh_w2[l], ln_ffn_g[l], ln_ffn_b[l], alpha)
    return xf.reshape(bsz, s, d).astype(x.dtype)
```

```python
import functools
import math

import jax
import jax.numpy as jnp
import numpy as np
from jax import lax
from jax.experimental import pallas as pl
from jax.experimental.pallas import tpu as pltpu

HEAD_DIM = 128
DIL_BRANCHES = ((128, 1), (512, 4), (2048, 16))
GRID_W = 64
NA_KH = 8
NA_KW = 16
T5_BUCKETS = 32
T5_MAX_DIST = 1024
N_GROUPS = 8
TOPK_GROUPS = 4
TOP_K = 8
ROUTED_SCALE = 2.5
LN_EPS = 1e-5
NEG = -1e30
LOG2E = math.log2(math.e)

V7X_VMEM_LIMIT = 56 * 1024 * 1024
LANES = 128
EXPERT_OUT_CHUNK = 512
DMA_UNROLL = 8
COMBINE_ROWS = 32

F32 = jnp.float32
BF16 = jnp.bfloat16
U32 = jnp.uint32
HIGH_HALF = 0xFFFF0000


def _params(sem, vmem=None):
    return pltpu.CompilerParams(dimension_semantics=sem, vmem_limit_bytes=vmem)


def _fit(tile, dim):
    if dim <= tile:
        return dim
    return max(t for t in range(128, tile + 1, 128) if dim % t == 0)


def _nt_dot(a, b, **kw):
    return lax.dot_general(a, b, (((1,), (1,)), ((), ())), preferred_element_type=F32, **kw)


def _layer_norm(z, g, b):
    mu = jnp.mean(z, axis=-1, keepdims=True)
    zc = z - mu
    var = jnp.mean(zc * zc, axis=-1, keepdims=True)
    return zc * lax.rsqrt(var + LN_EPS) * g + b


def _mm_body(x_ref, w_ref, o_ref):
    o_ref[...] = jnp.dot(x_ref[...], w_ref[...], preferred_element_type=F32).astype(o_ref.dtype)


def matmul(x, w, out_dtype, tm=512, tn=1024):
    m, k = x.shape
    n = w.shape[1]
    tm, tn = _fit(tm, m), _fit(tn, n)
    return pl.pallas_call(
        _mm_body,
        grid=(n // tn, m // tm),
        in_specs=[pl.BlockSpec((tm, k), lambda j, i: (i, 0)),
                  pl.BlockSpec((k, tn), lambda j, i: (0, j))],
        out_specs=pl.BlockSpec((tm, tn), lambda j, i: (i, j)),
        out_shape=jax.ShapeDtypeStruct((m, n), out_dtype),
        compiler_params=_params(("parallel", "parallel"), V7X_VMEM_LIMIT),
        name="in_proj",
    )(x, w)


def _proj_ln_body(a_ref, w_ref, res_ref, g_ref, b_ref, of_ref, ob_ref, *, alpha, nk, ln_rows):
    k = pl.program_id(1)

    @pl.when(k == 0)
    def _():
        of_ref[...] = jnp.dot(a_ref[...], w_ref[...], preferred_element_type=F32)

    @pl.when(k > 0)
    def _():
        of_ref[...] += jnp.dot(a_ref[...], w_ref[...], preferred_element_type=F32)

    @pl.when(k == nk - 1)
    def _():
        def chunk(c, carry):
            rows = pl.ds(pl.multiple_of(c * ln_rows, ln_rows), ln_rows)
            y = _layer_norm(alpha * res_ref[rows, :] + of_ref[rows, :], g_ref[...], b_ref[...])
            of_ref[rows, :] = y
            ob_ref[rows, :] = y.astype(BF16)
            return carry
        lax.fori_loop(0, of_ref.shape[0] // ln_rows, chunk, 0)


def proj_res_ln(a, w, res, g, b, alpha, tm=512, tk=256):
    m, kdim = a.shape
    d = w.shape[1]
    tm, tk = _fit(tm, m), _fit(tk, kdim)
    nk = kdim // tk
    return pl.pallas_call(
        functools.partial(_proj_ln_body, alpha=alpha, nk=nk, ln_rows=min(64, tm)),
        grid=(m // tm, nk),
        in_specs=[pl.BlockSpec((tm, tk), lambda i, k: (i, k)),
                  pl.BlockSpec((tk, d), lambda i, k: (k, 0)),
                  pl.BlockSpec((tm, d), lambda i, k: (i, 0)),
                  pl.BlockSpec((1, d), lambda i, k: (0, 0)),
                  pl.BlockSpec((1, d), lambda i, k: (0, 0))],
        out_specs=[pl.BlockSpec((tm, d), lambda i, k: (i, 0)),
                   pl.BlockSpec((tm, d), lambda i, k: (i, 0))],
        out_shape=[jax.ShapeDtypeStruct((m, d), F32), jax.ShapeDtypeStruct((m, d), BF16)],
        compiler_params=_params(("parallel", "arbitrary"), V7X_VMEM_LIMIT),
        name="out_proj_ln",
    )(a, w, res, g.reshape(1, d), b.reshape(1, d))


Q_SCALE = 2.0 ** math.floor(math.log2(HEAD_DIM ** -0.5))
SCORE_MUL = HEAD_DIM ** -0.5 / Q_SCALE * LOG2E


def _exp2_scores(q, k, bias2):
    t = _nt_dot(q * Q_SCALE, k) * SCORE_MUL + bias2
    return jnp.exp2(t - jnp.max(t, axis=-1, keepdims=True))


def _attend(q, k, v, bias2):
    e = _exp2_scores(q, k, bias2).astype(BF16)
    dv = v.shape[1]
    o = jnp.dot(e, jnp.concatenate([v, jnp.ones((v.shape[0], HEAD_DIM), v.dtype)], axis=-1),
                preferred_element_type=F32)
    return o[:, :dv] * (1.0 / o[:, dv:dv + 1])


def _attend_wide(q, k, v, bias2):
    e = _exp2_scores(q, k, bias2)
    o = jnp.dot(e.astype(BF16), v, preferred_element_type=F32)
    return o * (1.0 / jnp.sum(e, axis=-1, keepdims=True))


def _band_bias(band_ref, i, kb0, kb1, nq):
    return jnp.concatenate([band_ref[nq - 1 - i + kb] for kb in range(kb0, kb1)], axis=-1)


def _diff_body(dl_ref, gain_ref, q_ref, k_ref, v_ref, band_ref, o_ref, *, tq, lam_init):
    nq = k_ref.shape[0] // tq
    dl = dl_ref[...]
    lam = (jnp.exp(jnp.sum(dl[0:1] * dl[1:2], axis=-1, keepdims=True))
           - jnp.exp(jnp.sum(dl[2:3] * dl[3:4], axis=-1, keepdims=True)) + lam_init)
    for i in range(nq):
        bias2 = _band_bias(band_ref, i, 0, nq, nq)
        q = q_ref[i * tq:(i + 1) * tq, :]
        o0 = _attend_wide(q[:, :HEAD_DIM], k_ref[:, :HEAD_DIM], v_ref[...], bias2)
        o1 = _attend_wide(q[:, HEAD_DIM:], k_ref[:, HEAD_DIM:], v_ref[...], bias2)
        o = o0 - lam * o1
        o = o * lax.rsqrt(jnp.mean(o * o, axis=-1, keepdims=True) + LN_EPS) * gain_ref[...]
        o_ref[i * tq:(i + 1) * tq, :] = (o * (1.0 - lam_init)).astype(o_ref.dtype)


def _dil_body(q_ref, k_ref, v_ref, band_ref, o_ref, *, tq, reach):
    nq = k_ref.shape[0] // tq
    for i in range(nq):
        live = [kb for kb in range(nq) if kb == i or (abs(kb - i) - 1) * tq + 1 <= reach]
        kb0, kb1 = live[0], live[-1] + 1
        o = _attend(q_ref[i * tq:(i + 1) * tq, :], k_ref[kb0 * tq:kb1 * tq, :],
                    v_ref[kb0 * tq:kb1 * tq, :], _band_bias(band_ref, i, kb0, kb1, nq))
        o_ref[i * tq:(i + 1) * tq, :] = o.astype(o_ref.dtype)


def _na_body(q_ref, k_ref, v_ref, bias_ref, o_ref, *, rows_q, win):
    rows = q_ref.shape[0] // GRID_W
    nrb = rows // rows_q
    tq = rows_q * GRID_W
    for rb in range(nrb):
        w0 = min(max(rb * rows_q - NA_KH // 2, 0), rows - win) * GRID_W
        case = 0 if rb == 0 else (2 if rb == nrb - 1 else 1)
        o = _attend(q_ref[rb * tq:(rb + 1) * tq, :], k_ref[w0:w0 + win * GRID_W, :],
                    v_ref[w0:w0 + win * GRID_W, :], bias_ref[case])
        o_ref[rb * tq:(rb + 1) * tq, :] = o.astype(o_ref.dtype)


def _toeplitz(u, t):
    lead = u.shape[:-1]
    span = 2 * t - 1
    m = jnp.tile(u, (1,) * len(lead) + (t,))[..., :t * span].reshape(lead + (t, span))
    return m[..., t - 1:]


def _t5_rel_table(table, s):
    rel = jnp.arange(-(s - 1), s, dtype=jnp.int32)
    half = T5_BUCKETS // 2
    exact = half // 2
    n = jnp.abs(rel)
    log_ratio = jnp.log(jnp.maximum(n, 1).astype(F32) / exact) / math.log(T5_MAX_DIST / exact)
    large = jnp.minimum(exact + (log_ratio * (half - exact)).astype(jnp.int32), half - 1)
    bucket = jnp.where(rel > 0, half, 0) + jnp.where(n < exact, n, large)
    return table[bucket].astype(F32)


def _toeplitz_tiles(rel_tab, s, tq):
    h = rel_tab.shape[1]
    nq = s // tq
    r = jnp.concatenate([rel_tab.T, jnp.zeros((h, 1), F32)], axis=1).reshape(h, 2 * nq, tq)
    return _toeplitz(jnp.concatenate([r[:, :-1], r[:, 1:]], axis=-1), tq)


def _dilated_reach():
    return max((window // 2 // dilation) * dilation for window, dilation in DIL_BRANCHES)


def _dilated_log_multiplicity(s):
    rel = jnp.arange(-(s - 1), s, dtype=jnp.int32)
    cnt = jnp.zeros(rel.shape, F32)
    for window, dilation in DIL_BRANCHES:
        reach = (window // 2 // dilation) * dilation
        cnt = cnt + ((rel % dilation == 0) & (jnp.abs(rel) <= reach)).astype(F32)
    return jnp.where(cnt > 0, jnp.log(jnp.maximum(cnt, 1.0)), NEG)


def even_attention(h, bsz, s, d, t5_table, diff_lambda, diff_gain, lam_init, tq=256):
    n = bsz * s
    dh = HEAD_DIM
    h_diff = d // (4 * dh)
    h_dil = d // (2 * dh)
    nq = s // tq
    rel_tab = _t5_rel_table(t5_table, s)
    band_a = _toeplitz_tiles(rel_tab[:, :h_diff] * LOG2E, s, tq)
    band_b = _toeplitz_tiles((rel_tab[:, h_diff:] + _dilated_log_multiplicity(s)[:, None]) * LOG2E, s, tq)
    nb = 2 * nq - 1

    ya = pl.pallas_call(
        functools.partial(_diff_body, tq=tq, lam_init=lam_init),
        grid=(h_diff, bsz),
        in_specs=[pl.BlockSpec((4, dh), lambda hh, b: (0, 0)),
                  pl.BlockSpec((1, 2 * dh), lambda hh, b: (0, 0)),
                  pl.BlockSpec((s, 2 * dh), lambda hh, b: (b, hh)),
                  pl.BlockSpec((s, 2 * dh), lambda hh, b: (b, h_diff + hh)),
                  pl.BlockSpec((s, 2 * dh), lambda hh, b: (b, 2 * h_diff + hh)),
                  pl.BlockSpec((None, nb, tq, tq), lambda hh, b: (hh, 0, 0, 0))],
        out_specs=pl.BlockSpec((s, 2 * dh), lambda hh, b: (b, hh)),
        out_shape=jax.ShapeDtypeStruct((n, h_diff * 2 * dh), BF16),
        compiler_params=_params(("parallel", "parallel"), V7X_VMEM_LIMIT),
        name="diff_attn",
    )(diff_lambda.astype(F32), diff_gain.astype(F32).reshape(1, 2 * dh), h, h, h, band_a)

    c0 = 3 * h_diff * 2
    yb = pl.pallas_call(
        functools.partial(_dil_body, tq=tq, reach=_dilated_reach()),
        grid=(h_dil, bsz),
        in_specs=[pl.BlockSpec((s, dh), lambda hh, b: (b, c0 + hh)),
                  pl.BlockSpec((s, dh), lambda hh, b: (b, c0 + h_dil + hh)),
                  pl.BlockSpec((s, dh), lambda hh, b: (b, c0 + 2 * h_dil + hh)),
                  pl.BlockSpec((None, nb, tq, tq), lambda hh, b: (hh, 0, 0, 0))],
        out_specs=pl.BlockSpec((s, dh), lambda hh, b: (b, hh)),
        out_shape=jax.ShapeDtypeStruct((n, h_dil * dh), BF16),
        compiler_params=_params(("parallel", "parallel"), V7X_VMEM_LIMIT),
        name="dilated_attn",
    )(h, h, h, band_b)
    return jnp.concatenate([ya, yb], axis=-1)


def _na_bias_tiles(rpb, rows, rows_q, win):
    w = GRID_W
    kh = min(NA_KH, rows)
    col = jnp.arange(w, dtype=jnp.int32)
    cs = jnp.clip(col - NA_KW // 2, 0, w - NA_KW)
    col_mask = (col[None, :] >= cs[:, None]) & (col[None, :] < cs[:, None] + NA_KW)
    left = (w - 1) - (NA_KW - 1)
    u = jnp.pad(rpb.astype(F32), ((0, 0), (0, 0), (left, 2 * w - left - (2 * NA_KW - 1))),
                constant_values=NEG)
    cm = jnp.where(col_mask[None, None], _toeplitz(u, w), NEG)
    n_off = cm.shape[1]
    cm = jnp.concatenate([cm, jnp.full((rpb.shape[0], 1, w, w), NEG, F32)], axis=1)
    nrb = rows // rows_q
    block = []
    for rb in (0, 1, nrb - 1):
        r0 = rb * rows_q
        w0 = min(max(r0 - NA_KH // 2, 0), rows - win)
        block.append([])
        for qr in range(rows_q):
            r = r0 + qr
            rs = min(max(r - kh // 2, 0), rows - kh)
            block[-1].append([w0 + kj - r + NA_KH - 1 if rs <= w0 + kj < rs + kh else n_off
                              for kj in range(win)])
    tiles = cm[:, np.asarray(block, np.int32)]
    return tiles.transpose(0, 1, 2, 4, 3, 5).reshape(rpb.shape[0], 3, rows_q * w, win * w)


def odd_attention(h, bsz, s, d, rpb, rows_q=4, win=12):
    n = bsz * s
    dh = HEAD_DIM
    nh = d // dh
    rows = s // GRID_W
    nrb = rows // rows_q
    assert rows % rows_q == 0 and nrb >= 3 and rows >= win
    assert win >= rows_q + min(NA_KH, rows) - 1 and rows_q <= NA_KH // 2
    tq = rows_q * GRID_W
    bias = _na_bias_tiles(rpb, rows, rows_q, win) * LOG2E
    return pl.pallas_call(
        functools.partial(_na_body, rows_q=rows_q, win=win),
        grid=(nh, bsz),
        in_specs=[pl.BlockSpec((s, dh), lambda hh, b: (b, hh)),
                  pl.BlockSpec((s, dh), lambda hh, b: (b, nh + hh)),
                  pl.BlockSpec((s, dh), lambda hh, b: (b, 2 * nh + hh)),
                  pl.BlockSpec((None, 3, tq, win * GRID_W), lambda hh, b: (hh, 0, 0, 0))],
        out_specs=pl.BlockSpec((s, dh), lambda hh, b: (b, hh)),
        out_shape=jax.ShapeDtypeStruct((n, d), BF16),
        compiler_params=_params(("parallel", "parallel")),
        name="na_attn",
    )(h, h, h, bias)


def _router_body(x_ref, wr_ref, rb_ref, tri_ref, eidx_ref, wts_ref, rank_ref, cnt_ref, carry_ref):
    ne, tm = wr_ref.shape[0], x_ref.shape[0]
    gsz = ne // N_GROUPS
    ninf = -jnp.inf

    @pl.when(pl.program_id(0) == 0)
    def _():
        carry_ref[...] = jnp.zeros_like(carry_ref)

    logits = _nt_dot(wr_ref[...], x_ref[...], precision=lax.Precision.HIGHEST)
    scores = jax.nn.sigmoid(logits)
    sel = scores + rb_ref[...]

    g3 = sel.reshape(N_GROUPS, gsz, tm)
    i_in = lax.broadcasted_iota(jnp.int32, g3.shape, 1)
    m1 = jnp.max(g3, axis=1, keepdims=True)
    first = jnp.min(jnp.where(g3 == m1, i_in, gsz), axis=1, keepdims=True)
    m2 = jnp.max(jnp.where(i_in == first, ninf, g3), axis=1, keepdims=True)
    grp = m1 + m2

    i_g = lax.broadcasted_iota(jnp.int32, grp.shape, 0)
    keep = jnp.zeros(grp.shape, jnp.bool_)
    for _ in range(TOPK_GROUPS):
        gm = jnp.max(grp, axis=0, keepdims=True)
        gi = jnp.min(jnp.where(grp == gm, i_g, N_GROUPS), axis=0, keepdims=True)
        hit = i_g == gi
        keep = keep | hit
        grp = jnp.where(hit, ninf, grp)
    selm = jnp.where(keep, g3, ninf).reshape(ne, tm)

    i_e = lax.broadcasted_iota(jnp.int32, (ne, tm), 0)
    hits, eids, ws = [], [], []
    for _ in range(TOP_K):
        mx = jnp.max(selm, axis=0, keepdims=True)
        ei = jnp.min(jnp.where(selm == mx, i_e, ne), axis=0, keepdims=True)
        hit = i_e == ei
        ws.append(jnp.sum(jnp.where(hit, scores, 0.0), axis=0, keepdims=True))
        selm = jnp.where(hit, ninf, selm)
        hits.append(hit)
        eids.append(ei)
    wsum = ws[0]
    for wk in ws[1:]:
        wsum = wsum + wk

    msel = jnp.zeros((ne, tm), F32)
    for hit in hits:
        msel = msel + hit.astype(F32)
    incl = jnp.dot(msel.astype(BF16), tri_ref[...], preferred_element_type=F32)
    rank_ex = incl - msel + carry_ref[:, 0:1]
    ranks = [jnp.sum(jnp.where(hit, rank_ex, 0.0), axis=0, keepdims=True) for hit in hits]

    eidx_ref[...] = jnp.concatenate(eids, axis=0)
    wts_ref[...] = jnp.concatenate([wk / wsum * ROUTED_SCALE for wk in ws], axis=0)
    rank_ref[...] = jnp.concatenate(ranks, axis=0).astype(jnp.int32)
    carry_ref[...] = carry_ref[...] + jnp.sum(msel, axis=1, keepdims=True)
    cnt_ref[...] = carry_ref[...]


def router(x, w_router, router_bias, tm=512):
    n, d = x.shape
    ne = w_router.shape[1]
    tm = _fit(tm, n)
    tri = (jnp.arange(tm)[:, None] <= jnp.arange(tm)[None, :]).astype(BF16)
    kn = jax.ShapeDtypeStruct((TOP_K, n), jnp.int32)
    eidx, wts, rank, cnt = pl.pallas_call(
        _router_body,
        grid=(n // tm,),
        in_specs=[pl.BlockSpec((tm, d), lambda t: (t, 0)),
                  pl.BlockSpec((ne, d), lambda t: (0, 0)),
                  pl.BlockSpec((ne, 1), lambda t: (0, 0)),
                  pl.BlockSpec((tm, tm), lambda t: (0, 0))],
        out_specs=[pl.BlockSpec((TOP_K, tm), lambda t: (0, t)),
                   pl.BlockSpec((TOP_K, tm), lambda t: (0, t)),
                   pl.BlockSpec((TOP_K, tm), lambda t: (0, t)),
                   pl.BlockSpec((ne, 128), lambda t: (0, 0))],
        out_shape=[kn, jax.ShapeDtypeStruct((TOP_K, n), F32), kn,
                   jax.ShapeDtypeStruct((ne, 128), F32)],
        scratch_shapes=[pltpu.VMEM((ne, 128), F32)],
        compiler_params=_params(("arbitrary",)),
        name="router",
    )(x, w_router.T.astype(F32), router_bias.astype(F32).reshape(ne, 1), tri)
    return eidx, wts, rank, cnt[:, 0].astype(jnp.int32)


def _shared_body(x_ref, xb_ref, w13_ref, w2_ref, o_ref, *, alpha):
    f = w2_ref.shape[0]
    a = jnp.dot(xb_ref[...], w13_ref[...], preferred_element_type=F32)
    hs = jax.nn.silu(a[:, :f]) * a[:, f:]
    o_ref[...] = alpha * x_ref[...] + jnp.dot(hs.astype(BF16), w2_ref[...], preferred_element_type=F32)


def shared_ffn_residual(x, xb, ws13, ws2, alpha, tm=256):
    n, d = x.shape
    f = ws2.shape[0]
    tm = _fit(tm, n)
    tok = lambda i: (i, 0)
    const = lambda i: (0, 0)
    return pl.pallas_call(
        functools.partial(_shared_body, alpha=alpha),
        grid=(n // tm,),
        in_specs=[pl.BlockSpec((tm, d), tok), pl.BlockSpec((tm, d), tok),
                  pl.BlockSpec((d, 2 * f), const), pl.BlockSpec((f, d), const)],
        out_specs=pl.BlockSpec((tm, d), tok),
        out_shape=jax.ShapeDtypeStruct((n, d), F32),
        compiler_params=_params(("parallel",), V7X_VMEM_LIMIT),
        name="shared_ffn",
    )(x, xb, ws13, ws2)


def _slab_pitch(d):
    assert d % (2 * LANES) == 0
    return d // (2 * LANES) + 1


def _pack_bf16_pair(lo, hi):
    lo_bits = lax.bitcast_convert_type(lo.astype(BF16).astype(F32), U32)
    hi_bits = lax.bitcast_convert_type(hi.astype(BF16).astype(F32), U32)
    return (lo_bits >> 16) | (hi_bits & U32(HIGH_HALF))


def _unpack_bf16_pair(words):
    lo = lax.bitcast_convert_type(words << 16, F32)
    hi = lax.bitcast_convert_type(words & U32(HIGH_HALF), F32)
    return lo, hi


def _slab_copy(src, src_row, dst, dst_row, sem, pitch):
    return pltpu.make_async_copy(src.at[pl.ds(src_row * pitch, pitch), :],
                                 dst.at[pl.ds(dst_row * pitch, pitch), :], sem)


def _slab_block(s, rows, pitch):
    return (pl.ds(s, rows, stride=pitch), slice(None))


def _dispatch_body(pos_ref, x_ref, xs_hbm, slab, sem):
    tm, d = x_ref.shape
    pitch = _slab_pitch(d)
    half = d // 2
    for s in range(pitch - 1):
        slab[_slab_block(s, tm, pitch)] = _pack_bf16_pair(
            x_ref[:, s * LANES:(s + 1) * LANES], x_ref[:, half + s * LANES:half + (s + 1) * LANES])
    slab[_slab_block(pitch - 1, tm, pitch)] = jnp.zeros((tm, LANES), slab.dtype)
    for k in range(TOP_K):
        def issue(rb, c, k=k):
            for u in range(DMA_UNROLL):
                r = rb * DMA_UNROLL + u
                _slab_copy(slab, r, xs_hbm, pos_ref[k, r], sem, pitch).start()
            return c
        lax.fori_loop(0, tm // DMA_UNROLL, issue, 0)
    for k in range(TOP_K):
        pltpu.make_async_copy(slab, xs_hbm.at[pl.ds(0, tm * pitch), :], sem).wait()


def dispatch(x, pos_tiles):
    n, d = x.shape
    nt, _, tm = pos_tiles.shape
    pitch = _slab_pitch(d)
    return pl.pallas_call(
        _dispatch_body,
        grid=(nt,),
        in_specs=[pl.BlockSpec((None, TOP_K, tm), lambda i: (i, 0, 0), memory_space=pltpu.SMEM),
                  pl.BlockSpec((tm, d), lambda i: (i, 0))],
        out_specs=pl.BlockSpec(memory_space=pl.ANY),
        out_shape=jax.ShapeDtypeStruct((TOP_K * n * pitch, LANES), U32),
        scratch_shapes=[pltpu.VMEM((tm * pitch, LANES), U32), pltpu.SemaphoreType.DMA],
        compiler_params=_params(("arbitrary",)),
        name="dispatch",
    )(pos_tiles, x)


def _expert_body(tile_ref, exp_ref, lo_ref, hi_ref, first_ref, change_ref, next_ref,
                 xs_ref, w1_hbm, w3_hbm, w2_hbm, ys_ref,
                 st1_ref, st3_ref, st2_ref, w13_ref, w2_ref, sems, *, layer):
    it = pl.program_id(0)
    f, d = w2_ref.shape
    pitch = _slab_pitch(d)
    half = d // 2
    tm = xs_ref.shape[0] // pitch
    per_dot = EXPERT_OUT_CHUNK // LANES
    lo, hi = lo_ref[it], hi_ref[it]

    def staging_copies(e):
        return (pltpu.make_async_copy(w1_hbm.at[layer, e], st1_ref, sems.at[0]),
                pltpu.make_async_copy(w3_hbm.at[layer, e], st3_ref, sems.at[1]),
                pltpu.make_async_copy(w2_hbm.at[layer, e], st2_ref, sems.at[2]))

    @pl.when(it == 0)
    def _():
        for cp in staging_copies(exp_ref[0]):
            cp.start()

    @pl.when(change_ref[it] == 1)
    def _():
        for cp in staging_copies(exp_ref[it]):
            cp.wait()
        w13_ref[:, :f] = st1_ref[...].astype(BF16)
        w13_ref[:, f:] = st3_ref[...].astype(BF16)
        w2_ref[...] = st2_ref[...].astype(BF16)

        @pl.when(next_ref[it] >= 0)
        def _():
            for cp in staging_copies(next_ref[it]):
                cp.start()

    @pl.when(hi > lo)
    def _():
        pairs = [_unpack_bf16_pair(xs_ref[_slab_block(s, tm, pitch)]) for s in range(pitch - 1)]
        x = jnp.concatenate([p[0].astype(BF16) for p in pairs] + [p[1].astype(BF16) for p in pairs],
                            axis=-1)
        a = jnp.dot(x, w13_ref[...], preferred_element_type=F32)
        g = jax.nn.silu(a[:, :f]) * a[:, f:]
        row = tile_ref[it] * tm + lax.broadcasted_iota(jnp.int32, (tm, 1), 0)
        mine = (row >= lo) & (row < hi)
        g = jnp.where(mine, g, 0.0).astype(BF16)

        def out_words(c):
            c0 = c * EXPERT_OUT_CHUNK
            y_lo = jnp.dot(g, w2_ref[:, c0:c0 + EXPERT_OUT_CHUNK], preferred_element_type=F32)
            y_hi = jnp.dot(g, w2_ref[:, half + c0:half + c0 + EXPERT_OUT_CHUNK],
                           preferred_element_type=F32)
            return _pack_bf16_pair(y_lo, y_hi)

        @pl.when(first_ref[it] == 1)
        def _():
            for c in range(half // EXPERT_OUT_CHUNK):
                w = out_words(c)
                for j in range(per_dot):
                    ys_ref[_slab_block(c * per_dot + j, tm, pitch)] = w[:, j * LANES:(j + 1) * LANES]
            ys_ref[_slab_block(pitch - 1, tm, pitch)] = jnp.zeros((tm, LANES), U32)

        @pl.when(first_ref[it] == 0)
        def _():
            for c in range(half // EXPERT_OUT_CHUNK):
                w = out_words(c)
                for j in range(per_dot):
                    blk = _slab_block(c * per_dot + j, tm, pitch)
                    ys_ref[blk] = jnp.where(mine, w[:, j * LANES:(j + 1) * LANES], ys_ref[blk])


def expert_ffn(xs, w1, w3, w2, layer, items, tm):
    f, d = w2.shape[2:]
    pitch = _slab_pitch(d)
    assert (d // 2) % EXPERT_OUT_CHUNK == 0 and (tm * pitch) % 8 == 0
    n_items = items[0].shape[0]
    slab_tile = lambda i, t, *_: (t[i], 0)
    grid_spec = pltpu.PrefetchScalarGridSpec(
        num_scalar_prefetch=len(items),
        grid=(n_items,),
        in_specs=[pl.BlockSpec((tm * pitch, LANES), slab_tile),
                  pl.BlockSpec(memory_space=pl.ANY),
                  pl.BlockSpec(memory_space=pl.ANY),
                  pl.BlockSpec(memory_space=pl.ANY)],
        out_specs=pl.BlockSpec((tm * pitch, LANES), slab_tile),
        scratch_shapes=[pltpu.VMEM((d, f), F32), pltpu.VMEM((d, f), F32), pltpu.VMEM((f, d), F32),
                        pltpu.VMEM((d, 2 * f), BF16), pltpu.VMEM((f, d), BF16),
                        pltpu.SemaphoreType.DMA((3,))],
    )
    return pl.pallas_call(
        functools.partial(_expert_body, layer=layer),
        grid_spec=grid_spec,
        out_shape=jax.ShapeDtypeStruct(xs.shape, U32),
        compiler_params=_params(("arbitrary",), V7X_VMEM_LIMIT),
        name="expert_ffn",
    )(*items, xs, w1, w3, w2)


def _work_items(counts, p, tm):
    ne = counts.shape[0]
    nt = p // tm
    n_items = nt + ne - 1
    ends = jnp.cumsum(counts)
    starts = ends - counts
    t_first = starts // tm
    t_last = jnp.maximum(ends - 1, 0) // tm
    per = jnp.where(counts > 0, t_last - t_first + 1, 0)
    item_end = jnp.cumsum(per)
    item_start = item_end - per
    total = item_end[-1]
    i = jnp.arange(n_items, dtype=jnp.int32)
    ic = jnp.minimum(i, total - 1)
    e = jnp.sum((item_end[None, :] <= ic[:, None]).astype(jnp.int32), axis=1)
    onehot = (e[:, None] == jnp.arange(ne)[None, :]).astype(jnp.int32)
    pick = lambda a: jnp.sum(onehot * a[None, :], axis=1)
    tile = pick(t_first) + ic - pick(item_start)
    lo = jnp.maximum(pick(starts), tile * tm)
    hi = jnp.minimum(pick(ends), (tile + 1) * tm)
    live = i < total
    lo = jnp.where(live, lo, 0)
    hi = jnp.where(live, hi, 0)
    first = (live & (lo == tile * tm)).astype(jnp.int32)
    change = jnp.concatenate([jnp.ones((1,), jnp.int32), (e[1:] != e[:-1]).astype(jnp.int32)])
    ex = jnp.arange(ne)
    later = (ex[None, :] > ex[:, None]) & (per > 0)[None, :]
    next_of = jnp.min(jnp.where(later, ex[None, :], ne), axis=1)
    nxt = pick(jnp.where(next_of < ne, next_of, -1))
    as_i32 = lambda a: a.astype(jnp.int32)
    return as_i32(tile), as_i32(e), as_i32(lo), as_i32(hi), first, change, as_i32(nxt)


def _combine_body(pos_ref, base_ref, wt_ref, g_ref, b_ref, ys_hbm, of_ref, ob_ref, *scratch):
    bufs, acc_ref, sem = scratch[:TOP_K], scratch[TOP_K], scratch[TOP_K + 1]
    tm, d = base_ref.shape
    pitch = _slab_pitch(d)
    half = d // 2

    for k in range(TOP_K):
        def issue(rb, c, k=k):
            for u in range(DMA_UNROLL):
                r = rb * DMA_UNROLL + u
                _slab_copy(ys_hbm, pos_ref[k, r], bufs[k], r, sem, pitch).start()
            return c
        lax.fori_loop(0, tm // DMA_UNROLL, issue, 0)
    for k in range(TOP_K):
        pltpu.make_async_copy(ys_hbm.at[pl.ds(0, tm * pitch), :], bufs[k], sem).wait()

    for r0 in range(0, tm, COMBINE_ROWS):
        rows = slice(r0, r0 + COMBINE_ROWS)
        gates = [jnp.broadcast_to(wt_ref[rows, k:k + 1], (COMBINE_ROWS, LANES)) for k in range(TOP_K)]
        for s in range(pitch - 1):
            lo_cols = slice(s * LANES, (s + 1) * LANES)
            hi_cols = slice(half + s * LANES, half + (s + 1) * LANES)
            lo_acc = base_ref[rows, lo_cols]
            hi_acc = base_ref[rows, hi_cols]
            for k in range(TOP_K):
                lo, hi = _unpack_bf16_pair(bufs[k][pl.ds(r0 * pitch + s, COMBINE_ROWS, stride=pitch), :])
                lo_acc = lo_acc + gates[k] * lo
                hi_acc = hi_acc + gates[k] * hi
            acc_ref[rows, lo_cols] = lo_acc
            acc_ref[rows, hi_cols] = hi_acc
    y = _layer_norm(acc_ref[...], g_ref[...], b_ref[...])
    of_ref[...] = y
    ob_ref[...] = y.astype(BF16)


def combine_ln(base, ys, pos_tiles, wts_t, g, b):
    n, d = base.shape
    nt, _, tm = pos_tiles.shape
    pitch = _slab_pitch(d)
    tok = lambda i: (i, 0)
    const = lambda i: (0, 0)
    return pl.pallas_call(
        _combine_body,
        grid=(nt,),
        in_specs=[pl.BlockSpec((None, TOP_K, tm), lambda i: (i, 0, 0), memory_space=pltpu.SMEM),
                  pl.BlockSpec((tm, d), tok),
                  pl.BlockSpec((tm, TOP_K), tok),
                  pl.BlockSpec((1, d), const),
                  pl.BlockSpec((1, d), const),
                  pl.BlockSpec(memory_space=pl.ANY)],
        out_specs=[pl.BlockSpec((tm, d), tok), pl.BlockSpec((tm, d), tok)],
        out_shape=[jax.ShapeDtypeStruct((n, d), F32), jax.ShapeDtypeStruct((n, d), BF16)],
        scratch_shapes=[pltpu.VMEM((tm * pitch, LANES), U32) for _ in range(TOP_K)]
        + [pltpu.VMEM((tm, d), F32), pltpu.SemaphoreType.DMA],
        compiler_params=_params(("arbitrary",), V7X_VMEM_LIMIT),
        name="combine_ln",
    )(pos_tiles, base, wts_t, g.reshape(1, d), b.reshape(1, d), ys)


def moe_block(x, xb, w_router, router_bias, w1, w3, w2, layer, ws1, ws3, ws2, g, b, alpha,
              tm_tok=256, tm_exp=256):
    n, d = x.shape
    ne = w_router.shape[1]
    tm_tok, tm_exp = _fit(tm_tok, n), _fit(tm_exp, n)
    eidx, wts, rank, counts = router(x, w_router, router_bias)
    starts = jnp.cumsum(counts) - counts
    pos = rank + jnp.sum(jnp.where(eidx[..., None] == jnp.arange(ne), starts, 0), axis=-1)
    pos_tiles = pos.reshape(TOP_K, n // tm_tok, tm_tok).transpose(1, 0, 2)
    xs = dispatch(x, pos_tiles)
    items = _work_items(counts, TOP_K * n, tm_exp)
    ys = expert_ffn(xs, w1, w3, w2, layer, items, tm_exp)
    ws13 = jnp.concatenate([ws1.astype(BF16), ws3.astype(BF16)], axis=-1)
    base = shared_ffn_residual(x, xb, ws13, ws2.astype(BF16), alpha)
    return combine_ln(base, ys, pos_tiles, wts.T, g, b)


def kernel(x, t5_table, w_in_even, diff_lambda, diff_gain, w_out_even, w_in_odd, na_rpb, w_out_odd,
           ln_mix_g, ln_mix_b, router_w, router_bias, exp_w1, exp_w3, exp_w2, sh_w1, sh_w3, sh_w2,
           ln_ffn_g, ln_ffn_b):
    bsz, s, d = x.shape
    depth = ln_mix_g.shape[0]
    alpha = (2 * depth) ** 0.25
    xf = x.reshape(bsz * s, d).astype(F32)
    xb = xf.astype(BF16)
    for l in range(depth):
        j = l // 2
        if l % 2 == 0:
            h = matmul(xb, w_in_even[j].astype(BF16), BF16)
            lam_init = 0.8 - 0.6 * math.exp(-0.3 * l)
            att = even_attention(h, bsz, s, d, t5_table, diff_lambda[j], diff_gain[j], lam_init)
            w_out = w_out_even[j]
        else:
            h = matmul(xb, w_in_odd[j].astype(BF16), BF16)
            att = odd_attention(h, bsz, s, d, na_rpb[j])
            w_out = w_out_odd[j]
        xf, xb = proj_res_ln(att, w_out.astype(BF16), xf, ln_mix_g[l], ln_mix_b[l], alpha)
        xf, xb = moe_block(xf, xb, router_w[l], router_bias[l], exp_w1, exp_w3, exp_w2, l,
                           sh_w1[l], sh_w3[l], sh_w2[l], ln_ffn_g[l], ln_ffn_b[l], alpha)
    return xf.reshape(bsz, s, d).astype(x.dtype)
```

```python
import functools
import math

import jax
import jax.numpy as jnp
import numpy as np
from jax import lax
from jax.experimental import pallas as pl
from jax.experimental.pallas import tpu as pltpu

HEAD_DIM = 128
DIL_BRANCHES = ((128, 1), (512, 4), (2048, 16))
GRID_W = 64
NA_KH = 8
NA_KW = 16
T5_BUCKETS = 32
T5_MAX_DIST = 1024
N_GROUPS = 8
TOPK_GROUPS = 4
TOP_K = 8
ROUTED_SCALE = 2.5
LN_EPS = 1e-5
NEG = -1e30
LOG2E = math.log2(math.e)

V7X_VMEM_LIMIT = 56 * 1024 * 1024
LANES = 128
EXPERT_OUT_CHUNK = 512
DMA_UNROLL = 8
COMBINE_ROWS = 32

F32 = jnp.float32
BF16 = jnp.bfloat16
U32 = jnp.uint32
HIGH_HALF = 0xFFFF0000


def _params(sem, vmem=None):
    return pltpu.CompilerParams(dimension_semantics=sem, vmem_limit_bytes=vmem)


def _fit(tile, dim):
    if dim <= tile:
        return dim
    return max(t for t in range(128, tile + 1, 128) if dim % t == 0)


def _nt_dot(a, b, **kw):
    return lax.dot_general(a, b, (((1,), (1,)), ((), ())), preferred_element_type=F32, **kw)


def _layer_norm(z, g, b):
    mu = jnp.mean(z, axis=-1, keepdims=True)
    zc = z - mu
    var = jnp.mean(zc * zc, axis=-1, keepdims=True)
    return zc * lax.rsqrt(var + LN_EPS) * g + b


def _mm_body(x_ref, w_ref, o_ref, wb_ref):
    @pl.when(pl.program_id(1) == 0)
    def _():
        wb_ref[...] = w_ref[...].astype(BF16)

    o_ref[...] = jnp.dot(x_ref[...], wb_ref[...], preferred_element_type=F32).astype(o_ref.dtype)


def matmul(x, w, out_dtype, tm=512, tn=1024):
    m, k = x.shape
    n = w.shape[1]
    tm, tn = _fit(tm, m), _fit(tn, n)
    return pl.pallas_call(
        _mm_body,
        grid=(n // tn, m // tm),
        in_specs=[pl.BlockSpec((tm, k), lambda j, i: (i, 0)),
                  pl.BlockSpec((k, tn), lambda j, i: (0, j))],
        out_specs=pl.BlockSpec((tm, tn), lambda j, i: (i, j)),
        out_shape=jax.ShapeDtypeStruct((m, n), out_dtype),
        scratch_shapes=[pltpu.VMEM((k, tn), BF16)],
        compiler_params=_params(("parallel", "arbitrary"), V7X_VMEM_LIMIT),
        name="in_proj",
    )(x, w)


def _proj_ln_body(a_ref, w_ref, res_ref, g_ref, b_ref, of_ref, ob_ref, *, alpha, nk, ln_rows):
    k = pl.program_id(1)

    @pl.when(k == 0)
    def _():
        of_ref[...] = jnp.dot(a_ref[...], w_ref[...], preferred_element_type=F32)

    @pl.when(k > 0)
    def _():
        of_ref[...] += jnp.dot(a_ref[...], w_ref[...], preferred_element_type=F32)

    @pl.when(k == nk - 1)
    def _():
        def chunk(c, carry):
            rows = pl.ds(pl.multiple_of(c * ln_rows, ln_rows), ln_rows)
            y = _layer_norm(alpha * res_ref[rows, :] + of_ref[rows, :], g_ref[...], b_ref[...])
            of_ref[rows, :] = y
            ob_ref[rows, :] = y.astype(BF16)
            return carry
        lax.fori_loop(0, of_ref.shape[0] // ln_rows, chunk, 0)


def proj_res_ln(a, w, res, g, b, alpha, tm=512, tk=256):
    m, kdim = a.shape
    d = w.shape[1]
    tm, tk = _fit(tm, m), _fit(tk, kdim)
    nk = kdim // tk
    return pl.pallas_call(
        functools.partial(_proj_ln_body, alpha=alpha, nk=nk, ln_rows=min(64, tm)),
        grid=(m // tm, nk),
        in_specs=[pl.BlockSpec((tm, tk), lambda i, k: (i, k)),
                  pl.BlockSpec((tk, d), lambda i, k: (k, 0)),
                  pl.BlockSpec((tm, d), lambda i, k: (i, 0)),
                  pl.BlockSpec((1, d), lambda i, k: (0, 0)),
                  pl.BlockSpec((1, d), lambda i, k: (0, 0))],
        out_specs=[pl.BlockSpec((tm, d), lambda i, k: (i, 0)),
                   pl.BlockSpec((tm, d), lambda i, k: (i, 0))],
        out_shape=[jax.ShapeDtypeStruct((m, d), F32), jax.ShapeDtypeStruct((m, d), BF16)],
        compiler_params=_params(("parallel", "arbitrary"), V7X_VMEM_LIMIT),
        name="out_proj_ln",
    )(a, w, res, g.reshape(1, d), b.reshape(1, d))


Q_SCALE = 2.0 ** math.floor(math.log2(HEAD_DIM ** -0.5))
SCORE_MUL = HEAD_DIM ** -0.5 / Q_SCALE * LOG2E


def _exp2_scores(q, k, bias2):
    t = _nt_dot(q * Q_SCALE, k) * SCORE_MUL + bias2
    return jnp.exp2(t - jnp.max(t, axis=-1, keepdims=True))


def _attend(q, k, v, bias2):
    e = _exp2_scores(q, k, bias2).astype(BF16)
    dv = v.shape[1]
    o = jnp.dot(e, jnp.concatenate([v, jnp.ones((v.shape[0], HEAD_DIM), v.dtype)], axis=-1),
                preferred_element_type=F32)
    return o[:, :dv] * (1.0 / o[:, dv:dv + 1])


def _attend_wide(q, k, v, bias2):
    e = _exp2_scores(q, k, bias2)
    o = jnp.dot(e.astype(BF16), v, preferred_element_type=F32)
    return o * (1.0 / jnp.sum(e, axis=-1, keepdims=True))


def _band_bias(band_ref, i, kb0, kb1, nq):
    return jnp.concatenate([band_ref[nq - 1 - i + kb] for kb in range(kb0, kb1)], axis=-1)


def _fill_band(rel_ref, band_ref):
    nb, tq, _ = band_ref.shape

    @pl.when(pl.program_id(1) == 0)
    def _():
        for j in range(nb):
            window = jnp.broadcast_to(rel_ref[:, j * tq:(j + 2) * tq], (tq, 2 * tq))
            band_ref[j] = pltpu.roll(window, 1, 1, stride=1, stride_axis=0)[:, tq:]


def _diff_body(dl_ref, gain_ref, q_ref, k_ref, v_ref, rel_ref, o_ref, band_ref, *, tq, lam_init):
    _fill_band(rel_ref, band_ref)
    nq = k_ref.shape[0] // tq
    dl = dl_ref[...]
    lam = (jnp.exp(jnp.sum(dl[0:1] * dl[1:2], axis=-1, keepdims=True))
           - jnp.exp(jnp.sum(dl[2:3] * dl[3:4], axis=-1, keepdims=True)) + lam_init)
    for i in range(nq):
        bias2 = _band_bias(band_ref, i, 0, nq, nq)
        q = q_ref[i * tq:(i + 1) * tq, :]
        o0 = _attend_wide(q[:, :HEAD_DIM], k_ref[:, :HEAD_DIM], v_ref[...], bias2)
        o1 = _attend_wide(q[:, HEAD_DIM:], k_ref[:, HEAD_DIM:], v_ref[...], bias2)
        o = o0 - lam * o1
        o = o * lax.rsqrt(jnp.mean(o * o, axis=-1, keepdims=True) + LN_EPS) * gain_ref[...]
        o_ref[i * tq:(i + 1) * tq, :] = (o * (1.0 - lam_init)).astype(o_ref.dtype)


def _dil_body(q_ref, k_ref, v_ref, rel_ref, o_ref, band_ref, *, tq, reach):
    _fill_band(rel_ref, band_ref)
    nq = k_ref.shape[0] // tq
    for i in range(nq):
        live = [kb for kb in range(nq) if kb == i or (abs(kb - i) - 1) * tq + 1 <= reach]
        kb0, kb1 = live[0], live[-1] + 1
        o = _attend(q_ref[i * tq:(i + 1) * tq, :], k_ref[kb0 * tq:kb1 * tq, :],
                    v_ref[kb0 * tq:kb1 * tq, :], _band_bias(band_ref, i, kb0, kb1, nq))
        o_ref[i * tq:(i + 1) * tq, :] = o.astype(o_ref.dtype)


def _na_body(q_ref, k_ref, v_ref, bias_ref, o_ref, *, rows_q, win):
    rows = q_ref.shape[0] // GRID_W
    nrb = rows // rows_q
    tq = rows_q * GRID_W
    for rb in range(nrb):
        w0 = min(max(rb * rows_q - NA_KH // 2, 0), rows - win) * GRID_W
        case = 0 if rb == 0 else (2 if rb == nrb - 1 else 1)
        o = _attend(q_ref[rb * tq:(rb + 1) * tq, :], k_ref[w0:w0 + win * GRID_W, :],
                    v_ref[w0:w0 + win * GRID_W, :], bias_ref[case])
        o_ref[rb * tq:(rb + 1) * tq, :] = o.astype(o_ref.dtype)


def _toeplitz(u, t):
    lead = u.shape[:-1]
    span = 2 * t - 1
    m = jnp.tile(u, (1,) * len(lead) + (t,))[..., :t * span].reshape(lead + (t, span))
    return m[..., t - 1:]


def _t5_rel_table(table, s):
    rel = jnp.arange(-(s - 1), s, dtype=jnp.int32)
    half = T5_BUCKETS // 2
    exact = half // 2
    n = jnp.abs(rel)
    log_ratio = jnp.log(jnp.maximum(n, 1).astype(F32) / exact) / math.log(T5_MAX_DIST / exact)
    large = jnp.minimum(exact + (log_ratio * (half - exact)).astype(jnp.int32), half - 1)
    bucket = jnp.where(rel > 0, half, 0) + jnp.where(n < exact, n, large)
    return table[bucket].astype(F32)


def _dilated_reach():
    return max((window // 2 // dilation) * dilation for window, dilation in DIL_BRANCHES)


def _dilated_log_multiplicity(s):
    rel = jnp.arange(-(s - 1), s, dtype=jnp.int32)
    cnt = jnp.zeros(rel.shape, F32)
    for window, dilation in DIL_BRANCHES:
        reach = (window // 2 // dilation) * dilation
        cnt = cnt + ((rel % dilation == 0) & (jnp.abs(rel) <= reach)).astype(F32)
    return jnp.where(cnt > 0, jnp.log(jnp.maximum(cnt, 1.0)), NEG)


def even_attention(h, bsz, s, d, t5_table, diff_lambda, diff_gain, lam_init, tq=256):
    n = bsz * s
    dh = HEAD_DIM
    h_diff = d // (4 * dh)
    h_dil = d // (2 * dh)
    nq = s // tq
    rel_tab = _t5_rel_table(t5_table, s)

    def rel_rows(tab):
        return jnp.pad(tab.T * LOG2E, ((0, 0), (0, 1))).reshape(tab.shape[1], 1, 2 * s)

    rel_a = rel_rows(rel_tab[:, :h_diff])
    rel_b = rel_rows(rel_tab[:, h_diff:] + _dilated_log_multiplicity(s)[:, None])
    nb = 2 * nq - 1
    band_scratch = [pltpu.VMEM((nb, tq, tq), F32)]

    ya = pl.pallas_call(
        functools.partial(_diff_body, tq=tq, lam_init=lam_init),
        grid=(h_diff, bsz),
        in_specs=[pl.BlockSpec((4, dh), lambda hh, b: (0, 0)),
                  pl.BlockSpec((1, 2 * dh), lambda hh, b: (0, 0)),
                  pl.BlockSpec((s, 2 * dh), lambda hh, b: (b, hh)),
                  pl.BlockSpec((s, 2 * dh), lambda hh, b: (b, h_diff + hh)),
                  pl.BlockSpec((s, 2 * dh), lambda hh, b: (b, 2 * h_diff + hh)),
                  pl.BlockSpec((None, 1, 2 * s), lambda hh, b: (hh, 0, 0))],
        out_specs=pl.BlockSpec((s, 2 * dh), lambda hh, b: (b, hh)),
        out_shape=jax.ShapeDtypeStruct((n, h_diff * 2 * dh), BF16),
        scratch_shapes=band_scratch,
        compiler_params=_params(("parallel", "arbitrary"), V7X_VMEM_LIMIT),
        name="diff_attn",
    )(diff_lambda.astype(F32), diff_gain.astype(F32).reshape(1, 2 * dh), h, h, h, rel_a)

    c0 = 3 * h_diff * 2
    yb = pl.pallas_call(
        functools.partial(_dil_body, tq=tq, reach=_dilated_reach()),
        grid=(h_dil, bsz),
        in_specs=[pl.BlockSpec((s, dh), lambda hh, b: (b, c0 + hh)),
                  pl.BlockSpec((s, dh), lambda hh, b: (b, c0 + h_dil + hh)),
                  pl.BlockSpec((s, dh), lambda hh, b: (b, c0 + 2 * h_dil + hh)),
                  pl.BlockSpec((None, 1, 2 * s), lambda hh, b: (hh, 0, 0))],
        out_specs=pl.BlockSpec((s, dh), lambda hh, b: (b, hh)),
        out_shape=jax.ShapeDtypeStruct((n, h_dil * dh), BF16),
        scratch_shapes=band_scratch,
        compiler_params=_params(("parallel", "arbitrary"), V7X_VMEM_LIMIT),
        name="dilated_attn",
    )(h, h, h, rel_b)
    return jnp.concatenate([ya, yb], axis=-1)


def _na_bias_tiles(rpb, rows, rows_q, win):
    w = GRID_W
    kh = min(NA_KH, rows)
    col = jnp.arange(w, dtype=jnp.int32)
    cs = jnp.clip(col - NA_KW // 2, 0, w - NA_KW)
    col_mask = (col[None, :] >= cs[:, None]) & (col[None, :] < cs[:, None] + NA_KW)
    left = (w - 1) - (NA_KW - 1)
    u = jnp.pad(rpb.astype(F32), ((0, 0), (0, 0), (left, 2 * w - left - (2 * NA_KW - 1))),
                constant_values=NEG)
    cm = jnp.where(col_mask[None, None], _toeplitz(u, w), NEG)
    n_off = cm.shape[1]
    cm = jnp.concatenate([cm, jnp.full((rpb.shape[0], 1, w, w), NEG, F32)], axis=1)
    nrb = rows // rows_q
    block = []
    for rb in (0, 1, nrb - 1):
        r0 = rb * rows_q
        w0 = min(max(r0 - NA_KH // 2, 0), rows - win)
        block.append([])
        for qr in range(rows_q):
            r = r0 + qr
            rs = min(max(r - kh // 2, 0), rows - kh)
            block[-1].append([w0 + kj - r + NA_KH - 1 if rs <= w0 + kj < rs + kh else n_off
                              for kj in range(win)])
    tiles = cm[:, np.asarray(block, np.int32)]
    return tiles.transpose(0, 1, 2, 4, 3, 5).reshape(rpb.shape[0], 3, rows_q * w, win * w)


def odd_attention(h, bsz, s, d, rpb, rows_q=4, win=12):
    n = bsz * s
    dh = HEAD_DIM
    nh = d // dh
    rows = s // GRID_W
    nrb = rows // rows_q
    assert rows % rows_q == 0 and nrb >= 3 and rows >= win
    assert win >= rows_q + min(NA_KH, rows) - 1 and rows_q <= NA_KH // 2
    tq = rows_q * GRID_W
    bias = _na_bias_tiles(rpb, rows, rows_q, win) * LOG2E
    return pl.pallas_call(
        functools.partial(_na_body, rows_q=rows_q, win=win),
        grid=(nh, bsz),
        in_specs=[pl.BlockSpec((s, dh), lambda hh, b: (b, hh)),
                  pl.BlockSpec((s, dh), lambda hh, b: (b, nh + hh)),
                  pl.BlockSpec((s, dh), lambda hh, b: (b, 2 * nh + hh)),
                  pl.BlockSpec((None, 3, tq, win * GRID_W), lambda hh, b: (hh, 0, 0, 0))],
        out_specs=pl.BlockSpec((s, dh), lambda hh, b: (b, hh)),
        out_shape=jax.ShapeDtypeStruct((n, d), BF16),
        compiler_params=_params(("parallel", "parallel")),
        name="na_attn",
    )(h, h, h, bias)


def _router_body(x_ref, wr_ref, rb_ref, tri_ref, eidx_ref, wts_ref, rank_ref, cnt_ref, carry_ref):
    ne, tm = wr_ref.shape[0], x_ref.shape[0]
    gsz = ne // N_GROUPS
    ninf = -jnp.inf

    @pl.when(pl.program_id(0) == 0)
    def _():
        carry_ref[...] = jnp.zeros_like(carry_ref)

    wr = wr_ref[...]
    x = x_ref[...]
    w_hi = wr.astype(BF16)
    w_lo = (wr - w_hi.astype(F32)).astype(BF16)
    x_hi = x.astype(BF16)
    x_lo = (x - x_hi.astype(F32)).astype(BF16)
    logits = _nt_dot(w_hi, x_hi) + (_nt_dot(w_hi, x_lo) + _nt_dot(w_lo, x_hi))
    scores = jax.nn.sigmoid(logits)
    sel = scores + rb_ref[...]

    g3 = sel.reshape(N_GROUPS, gsz, tm)
    i_in = lax.broadcasted_iota(jnp.int32, g3.shape, 1)
    m1 = jnp.max(g3, axis=1, keepdims=True)
    first = jnp.min(jnp.where(g3 == m1, i_in, gsz), axis=1, keepdims=True)
    m2 = jnp.max(jnp.where(i_in == first, ninf, g3), axis=1, keepdims=True)
    grp = m1 + m2

    i_g = lax.broadcasted_iota(jnp.int32, grp.shape, 0)
    keep = jnp.zeros(grp.shape, jnp.bool_)
    for _ in range(TOPK_GROUPS):
        gm = jnp.max(grp, axis=0, keepdims=True)
        gi = jnp.min(jnp.where(grp == gm, i_g, N_GROUPS), axis=0, keepdims=True)
        hit = i_g == gi
        keep = keep | hit
        grp = jnp.where(hit, ninf, grp)
    selm = jnp.where(keep, g3, ninf).reshape(ne, tm)

    i_e = lax.broadcasted_iota(jnp.int32, (ne, tm), 0)
    hits, eids, ws = [], [], []
    for _ in range(TOP_K):
        mx = jnp.max(selm, axis=0, keepdims=True)
        ei = jnp.min(jnp.where(selm == mx, i_e, ne), axis=0, keepdims=True)
        hit = i_e == ei
        ws.append(jnp.sum(jnp.where(hit, scores, 0.0), axis=0, keepdims=True))
        selm = jnp.where(hit, ninf, selm)
        hits.append(hit)
        eids.append(ei)
    wsum = ws[0]
    for wk in ws[1:]:
        wsum = wsum + wk

    msel = jnp.zeros((ne, tm), F32)
    for hit in hits:
        msel = msel + hit.astype(F32)
    incl = jnp.dot(msel.astype(BF16), tri_ref[...], preferred_element_type=F32)
    rank_ex = incl - msel + carry_ref[:, 0:1]
    ranks = [jnp.sum(jnp.where(hit, rank_ex, 0.0), axis=0, keepdims=True) for hit in hits]

    eidx_ref[...] = jnp.concatenate(eids, axis=0)
    wts_ref[...] = jnp.concatenate([wk / wsum * ROUTED_SCALE for wk in ws], axis=0)
    rank_ref[...] = jnp.concatenate(ranks, axis=0).astype(jnp.int32)
    carry_ref[...] = carry_ref[...] + jnp.sum(msel, axis=1, keepdims=True)
    cnt_ref[...] = carry_ref[...]


def router(x, w_router, router_bias, tm=512):
    n, d = x.shape
    ne = w_router.shape[1]
    tm = _fit(tm, n)
    tri = (jnp.arange(tm)[:, None] <= jnp.arange(tm)[None, :]).astype(BF16)
    kn = jax.ShapeDtypeStruct((TOP_K, n), jnp.int32)
    eidx, wts, rank, cnt = pl.pallas_call(
        _router_body,
        grid=(n // tm,),
        in_specs=[pl.BlockSpec((tm, d), lambda t: (t, 0)),
                  pl.BlockSpec((ne, d), lambda t: (0, 0)),
                  pl.BlockSpec((ne, 1), lambda t: (0, 0)),
                  pl.BlockSpec((tm, tm), lambda t: (0, 0))],
        out_specs=[pl.BlockSpec((TOP_K, tm), lambda t: (0, t)),
                   pl.BlockSpec((TOP_K, tm), lambda t: (0, t)),
                   pl.BlockSpec((TOP_K, tm), lambda t: (0, t)),
                   pl.BlockSpec((ne, 128), lambda t: (0, 0))],
        out_shape=[kn, jax.ShapeDtypeStruct((TOP_K, n), F32), kn,
                   jax.ShapeDtypeStruct((ne, 128), F32)],
        scratch_shapes=[pltpu.VMEM((ne, 128), F32)],
        compiler_params=_params(("arbitrary",)),
        name="router",
    )(x, w_router.T.astype(F32), router_bias.astype(F32).reshape(ne, 1), tri)
    return eidx, wts, rank, cnt[:, 0].astype(jnp.int32)


def _shared_body(x_ref, xb_ref, w13_ref, w2_ref, o_ref, *, alpha):
    f = w2_ref.shape[0]
    a = jnp.dot(xb_ref[...], w13_ref[...], preferred_element_type=F32)
    hs = jax.nn.silu(a[:, :f]) * a[:, f:]
    o_ref[...] = alpha * x_ref[...] + jnp.dot(hs.astype(BF16), w2_ref[...], preferred_element_type=F32)


def shared_ffn_residual(x, xb, ws13, ws2, alpha, tm=256):
    n, d = x.shape
    f = ws2.shape[0]
    tm = _fit(tm, n)
    tok = lambda i: (i, 0)
    const = lambda i: (0, 0)
    return pl.pallas_call(
        functools.partial(_shared_body, alpha=alpha),
        grid=(n // tm,),
        in_specs=[pl.BlockSpec((tm, d), tok), pl.BlockSpec((tm, d), tok),
                  pl.BlockSpec((d, 2 * f), const), pl.BlockSpec((f, d), const)],
        out_specs=pl.BlockSpec((tm, d), tok),
        out_shape=jax.ShapeDtypeStruct((n, d), F32),
        compiler_params=_params(("parallel",), V7X_VMEM_LIMIT),
        name="shared_ffn",
    )(x, xb, ws13, ws2)


def _slab_pitch(d):
    assert d % (2 * LANES) == 0
    return d // (2 * LANES) + 1


def _pack_bf16_pair(lo, hi):
    lo_bits = lax.bitcast_convert_type(lo.astype(BF16).astype(F32), U32)
    hi_bits = lax.bitcast_convert_type(hi.astype(BF16).astype(F32), U32)
    return (lo_bits >> 16) | (hi_bits & U32(HIGH_HALF))


def _unpack_bf16_pair(words):
    lo = lax.bitcast_convert_type(words << 16, F32)
    hi = lax.bitcast_convert_type(words & U32(HIGH_HALF), F32)
    return lo, hi


def _slab_copy(src, src_start, dst, dst_start, sem, pitch):
    return pltpu.make_async_copy(src.at[pl.ds(src_start, pitch), :],
                                 dst.at[pl.ds(dst_start, pitch), :], sem)


def _slab_block(s, rows, pitch):
    return (pl.ds(s, rows, stride=pitch), slice(None))


def _dispatch_body(pos_ref, x_ref, xs_hbm, slab, sem):
    tm, d = x_ref.shape
    pitch = _slab_pitch(d)
    half = d // 2
    for s in range(pitch - 1):
        slab[_slab_block(s, tm, pitch)] = _pack_bf16_pair(
            x_ref[:, s * LANES:(s + 1) * LANES], x_ref[:, half + s * LANES:half + (s + 1) * LANES])
    slab[_slab_block(pitch - 1, tm, pitch)] = jnp.zeros((tm, LANES), slab.dtype)
    for k in range(TOP_K):
        def issue(rb, c, k=k):
            for u in range(DMA_UNROLL):
                r = rb * DMA_UNROLL + u
                _slab_copy(slab, r * pitch, xs_hbm, pos_ref[k, r], sem, pitch).start()
            return c
        lax.fori_loop(0, tm // DMA_UNROLL, issue, 0)
    for k in range(TOP_K):
        pltpu.make_async_copy(slab, xs_hbm.at[pl.ds(0, tm * pitch), :], sem).wait()


def dispatch(x, pos_tiles):
    n, d = x.shape
    nt, _, tm = pos_tiles.shape
    pitch = _slab_pitch(d)
    return pl.pallas_call(
        _dispatch_body,
        grid=(nt,),
        in_specs=[pl.BlockSpec((None, TOP_K, tm), lambda i: (i, 0, 0), memory_space=pltpu.SMEM),
                  pl.BlockSpec((tm, d), lambda i: (i, 0))],
        out_specs=pl.BlockSpec(memory_space=pl.ANY),
        out_shape=jax.ShapeDtypeStruct((TOP_K * n * pitch, LANES), U32),
        scratch_shapes=[pltpu.VMEM((tm * pitch, LANES), U32), pltpu.SemaphoreType.DMA],
        compiler_params=_params(("arbitrary",)),
        name="dispatch",
    )(pos_tiles, x)


def _expert_body(tile_ref, exp_ref, lo_ref, hi_ref, first_ref, change_ref, next_ref,
                 xs_ref, w1_hbm, w3_hbm, w2_hbm, ys_ref,
                 st1_ref, st3_ref, st2_ref, w13_ref, w2_ref, sems, *, layer):
    it = pl.program_id(0)
    f, d = w2_ref.shape
    pitch = _slab_pitch(d)
    half = d // 2
    tm = xs_ref.shape[0] // pitch
    per_dot = EXPERT_OUT_CHUNK // LANES
    lo, hi = lo_ref[it], hi_ref[it]

    def staging_copies(e):
        return (pltpu.make_async_copy(w1_hbm.at[layer, e], st1_ref, sems.at[0]),
                pltpu.make_async_copy(w3_hbm.at[layer, e], st3_ref, sems.at[1]),
                pltpu.make_async_copy(w2_hbm.at[layer, e], st2_ref, sems.at[2]))

    @pl.when(it == 0)
    def _():
        for cp in staging_copies(exp_ref[0]):
            cp.start()

    @pl.when(change_ref[it] == 1)
    def _():
        for cp in staging_copies(exp_ref[it]):
            cp.wait()
        w13_ref[:, :f] = st1_ref[...].astype(BF16)
        w13_ref[:, f:] = st3_ref[...].astype(BF16)
        w2_ref[...] = st2_ref[...].astype(BF16)

        @pl.when(next_ref[it] >= 0)
        def _():
            for cp in staging_copies(next_ref[it]):
                cp.start()

    @pl.when(hi > lo)
    def _():
        pairs = [_unpack_bf16_pair(xs_ref[_slab_block(s, tm, pitch)]) for s in range(pitch - 1)]
        x = jnp.concatenate([p[0].astype(BF16) for p in pairs] + [p[1].astype(BF16) for p in pairs],
                            axis=-1)
        a = jnp.dot(x, w13_ref[...], preferred_element_type=F32)
        g = jax.nn.silu(a[:, :f]) * a[:, f:]
        row = tile_ref[it] * tm + lax.broadcasted_iota(jnp.int32, (tm, 1), 0)
        mine = (row >= lo) & (row < hi)
        g = jnp.where(mine, g, 0.0).astype(BF16)

        def out_words(c):
            c0 = c * EXPERT_OUT_CHUNK
            y_lo = jnp.dot(g, w2_ref[:, c0:c0 + EXPERT_OUT_CHUNK], preferred_element_type=F32)
            y_hi = jnp.dot(g, w2_ref[:, half + c0:half + c0 + EXPERT_OUT_CHUNK],
                           preferred_element_type=F32)
            return _pack_bf16_pair(y_lo, y_hi)

        @pl.when(first_ref[it] == 1)
        def _():
            for c in range(half // EXPERT_OUT_CHUNK):
                w = out_words(c)
                for j in range(per_dot):
                    ys_ref[_slab_block(c * per_dot + j, tm, pitch)] = w[:, j * LANES:(j + 1) * LANES]
            ys_ref[_slab_block(pitch - 1, tm, pitch)] = jnp.zeros((tm, LANES), U32)

        @pl.when(first_ref[it] == 0)
        def _():
            for c in range(half // EXPERT_OUT_CHUNK):
                w = out_words(c)
                for j in range(per_dot):
                    blk = _slab_block(c * per_dot + j, tm, pitch)
                    ys_ref[blk] = jnp.where(mine, w[:, j * LANES:(j + 1) * LANES], ys_ref[blk])


def expert_ffn(xs, w1, w3, w2, layer, items, tm):
    f, d = w2.shape[2:]
    pitch = _slab_pitch(d)
    assert (d // 2) % EXPERT_OUT_CHUNK == 0 and (tm * pitch) % 8 == 0
    n_items = items[0].shape[0]
    slab_tile = lambda i, t, *_: (t[i], 0)
    grid_spec = pltpu.PrefetchScalarGridSpec(
        num_scalar_prefetch=len(items),
        grid=(n_items,),
        in_specs=[pl.BlockSpec((tm * pitch, LANES), slab_tile),
                  pl.BlockSpec(memory_space=pl.ANY),
                  pl.BlockSpec(memory_space=pl.ANY),
                  pl.BlockSpec(memory_space=pl.ANY)],
        out_specs=pl.BlockSpec((tm * pitch, LANES), slab_tile),
        scratch_shapes=[pltpu.VMEM((d, f), F32), pltpu.VMEM((d, f), F32), pltpu.VMEM((f, d), F32),
                        pltpu.VMEM((d, 2 * f), BF16), pltpu.VMEM((f, d), BF16),
                        pltpu.SemaphoreType.DMA((3,))],
    )
    return pl.pallas_call(
        functools.partial(_expert_body, layer=layer),
        grid_spec=grid_spec,
        out_shape=jax.ShapeDtypeStruct(xs.shape, U32),
        compiler_params=_params(("arbitrary",), V7X_VMEM_LIMIT),
        name="expert_ffn",
    )(*items, xs, w1, w3, w2)


def _work_items(counts, p, tm):
    ne = counts.shape[0]
    nt = p // tm
    n_items = nt + ne - 1
    ends = jnp.cumsum(counts)
    starts = ends - counts
    t_first = starts // tm
    t_last = jnp.maximum(ends - 1, 0) // tm
    per = jnp.where(counts > 0, t_last - t_first + 1, 0)
    item_end = jnp.cumsum(per)
    item_start = item_end - per
    total = item_end[-1]
    i = jnp.arange(n_items, dtype=jnp.int32)
    ic = jnp.minimum(i, total - 1)
    e = jnp.sum((item_end[None, :] <= ic[:, None]).astype(jnp.int32), axis=1)
    onehot = (e[:, None] == jnp.arange(ne)[None, :]).astype(jnp.int32)
    pick = lambda a: jnp.sum(onehot * a[None, :], axis=1)
    tile = pick(t_first) + ic - pick(item_start)
    lo = jnp.maximum(pick(starts), tile * tm)
    hi = jnp.minimum(pick(ends), (tile + 1) * tm)
    live = i < total
    lo = jnp.where(live, lo, 0)
    hi = jnp.where(live, hi, 0)
    first = (live & (lo == tile * tm)).astype(jnp.int32)
    change = jnp.concatenate([jnp.ones((1,), jnp.int32), (e[1:] != e[:-1]).astype(jnp.int32)])
    ex = jnp.arange(ne)
    later = (ex[None, :] > ex[:, None]) & (per > 0)[None, :]
    next_of = jnp.min(jnp.where(later, ex[None, :], ne), axis=1)
    nxt = pick(jnp.where(next_of < ne, next_of, -1))
    as_i32 = lambda a: a.astype(jnp.int32)
    return as_i32(tile), as_i32(e), as_i32(lo), as_i32(hi), first, change, as_i32(nxt)


def _combine_body(pos_ref, base_ref, wt_ref, g_ref, b_ref, ys_hbm, of_ref, ob_ref, *scratch):
    bufs, acc_ref, sem = scratch[:TOP_K], scratch[TOP_K], scratch[TOP_K + 1]
    tm, d = base_ref.shape
    pitch = _slab_pitch(d)
    half = d // 2

    for k in range(TOP_K):
        def issue(rb, c, k=k):
            for u in range(DMA_UNROLL):
                r = rb * DMA_UNROLL + u
                _slab_copy(ys_hbm, pos_ref[k, r], bufs[k], r * pitch, sem, pitch).start()
            return c
        lax.fori_loop(0, tm // DMA_UNROLL, issue, 0)
    for k in range(TOP_K):
        pltpu.make_async_copy(ys_hbm.at[pl.ds(0, tm * pitch), :], bufs[k], sem).wait()

    for r0 in range(0, tm, COMBINE_ROWS):
        rows = slice(r0, r0 + COMBINE_ROWS)
        gates = [jnp.broadcast_to(wt_ref[rows, k:k + 1], (COMBINE_ROWS, LANES)) for k in range(TOP_K)]
        for s in range(pitch - 1):
            lo_cols = slice(s * LANES, (s + 1) * LANES)
            hi_cols = slice(half + s * LANES, half + (s + 1) * LANES)
            lo_acc = base_ref[rows, lo_cols]
            hi_acc = base_ref[rows, hi_cols]
            for k in range(TOP_K):
                lo, hi = _unpack_bf16_pair(bufs[k][pl.ds(r0 * pitch + s, COMBINE_ROWS, stride=pitch), :])
                lo_acc = lo_acc + gates[k] * lo
                hi_acc = hi_acc + gates[k] * hi
            acc_ref[rows, lo_cols] = lo_acc
            acc_ref[rows, hi_cols] = hi_acc
    y = _layer_norm(acc_ref[...], g_ref[...], b_ref[...])
    of_ref[...] = y
    ob_ref[...] = y.astype(BF16)


def combine_ln(base, ys, pos_tiles, wts_t, g, b):
    n, d = base.shape
    nt, _, tm = pos_tiles.shape
    pitch = _slab_pitch(d)
    tok = lambda i: (i, 0)
    const = lambda i: (0, 0)
    return pl.pallas_call(
        _combine_body,
        grid=(nt,),
        in_specs=[pl.BlockSpec((None, TOP_K, tm), lambda i: (i, 0, 0), memory_space=pltpu.SMEM),
                  pl.BlockSpec((tm, d), tok),
                  pl.BlockSpec((tm, TOP_K), tok),
                  pl.BlockSpec((1, d), const),
                  pl.BlockSpec((1, d), const),
                  pl.BlockSpec(memory_space=pl.ANY)],
        out_specs=[pl.BlockSpec((tm, d), tok), pl.BlockSpec((tm, d), tok)],
        out_shape=[jax.ShapeDtypeStruct((n, d), F32), jax.ShapeDtypeStruct((n, d), BF16)],
        scratch_shapes=[pltpu.VMEM((tm * pitch, LANES), U32) for _ in range(TOP_K)]
        + [pltpu.VMEM((tm, d), F32), pltpu.SemaphoreType.DMA],
        compiler_params=_params(("arbitrary",), V7X_VMEM_LIMIT),
        name="combine_ln",
    )(pos_tiles, base, wts_t, g.reshape(1, d), b.reshape(1, d), ys)


def moe_block(x, xb, w_router, router_bias, w1, w3, w2, layer, ws1, ws3, ws2, g, b, alpha,
              tm_tok=256, tm_exp=256):
    n, d = x.shape
    ne = w_router.shape[1]
    tm_tok, tm_exp = _fit(tm_tok, n), _fit(tm_exp, n)
    eidx, wts, rank, counts = router(x, w_router, router_bias)
    starts = jnp.cumsum(counts) - counts
    pos = rank + jnp.sum(jnp.where(eidx[..., None] == jnp.arange(ne), starts, 0), axis=-1)
    pos_tiles = (pos * _slab_pitch(d)).reshape(TOP_K, n // tm_tok, tm_tok).transpose(1, 0, 2)
    xs = dispatch(x, pos_tiles)
    items = _work_items(counts, TOP_K * n, tm_exp)
    ys = expert_ffn(xs, w1, w3, w2, layer, items, tm_exp)
    ws13 = jnp.concatenate([ws1.astype(BF16), ws3.astype(BF16)], axis=-1)
    base = shared_ffn_residual(x, xb, ws13, ws2.astype(BF16), alpha)
    return combine_ln(base, ys, pos_tiles, wts.T, g, b)


def kernel(x, t5_table, w_in_even, diff_lambda, diff_gain, w_out_even, w_in_odd, na_rpb, w_out_odd,
           ln_mix_g, ln_mix_b, router_w, router_bias, exp_w1, exp_w3, exp_w2, sh_w1, sh_w3, sh_w2,
           ln_ffn_g, ln_ffn_b):
    bsz, s, d = x.shape
    depth = ln_mix_g.shape[0]
    alpha = (2 * depth) ** 0.25
    xf = x.reshape(bsz * s, d).astype(F32)
    xb = xf.astype(BF16)
    for l in range(depth):
        j = l // 2
        if l % 2 == 0:
            h = matmul(xb, w_in_even[j], BF16)
            lam_init = 0.8 - 0.6 * math.exp(-0.3 * l)
            att = even_attention(h, bsz, s, d, t5_table, diff_lambda[j], diff_gain[j], lam_init)
            w_out = w_out_even[j]
        else:
            h = matmul(xb, w_in_odd[j], BF16)
            att = odd_attention(h, bsz, s, d, na_rpb[j])
            w_out = w_out_odd[j]
        xf, xb = proj_res_ln(att, w_out.astype(BF16), xf, ln_mix_g[l], ln_mix_b[l], alpha)
        xf, xb = moe_block(xf, xb, router_w[l], router_bias[l], exp_w1, exp_w3, exp_w2, l,
                           sh_w1[l], sh_w3[l], sh_w2[l], ln_ffn_g[l], ln_ffn_b[l], alpha)
    return xf.reshape(bsz, s, d).astype(x.dtype)
```

```python
import functools
import math

import jax
import jax.numpy as jnp
import numpy as np
from jax import lax
from jax.experimental import pallas as pl
from jax.experimental.pallas import tpu as pltpu

HEAD_DIM = 128
DIL_BRANCHES = ((128, 1), (512, 4), (2048, 16))
GRID_W = 64
NA_KH = 8
NA_KW = 16
T5_BUCKETS = 32
T5_MAX_DIST = 1024
N_GROUPS = 8
TOPK_GROUPS = 4
TOP_K = 8
ROUTED_SCALE = 2.5
LN_EPS = 1e-5
NEG = -1e30
LOG2E = math.log2(math.e)

V7X_VMEM_LIMIT = 56 * 1024 * 1024
LANES = 128
EXPERT_OUT_CHUNK = 512
DMA_UNROLL = 8
COMBINE_ROWS = 32

F32 = jnp.float32
BF16 = jnp.bfloat16
U32 = jnp.uint32
HIGH_HALF = 0xFFFF0000


def _params(sem, vmem=None):
    return pltpu.CompilerParams(dimension_semantics=sem, vmem_limit_bytes=vmem)


def _fit(tile, dim):
    if dim <= tile:
        return dim
    return max(t for t in range(128, tile + 1, 128) if dim % t == 0)


def _nt_dot(a, b, **kw):
    return lax.dot_general(a, b, (((1,), (1,)), ((), ())), preferred_element_type=F32, **kw)


def _layer_norm(z, g, b):
    mu = jnp.mean(z, axis=-1, keepdims=True)
    zc = z - mu
    var = jnp.mean(zc * zc, axis=-1, keepdims=True)
    return zc * lax.rsqrt(var + LN_EPS) * g + b


def _mm_body(x_ref, w_ref, o_ref, wb_ref):
    @pl.when(pl.program_id(1) == 0)
    def _():
        wb_ref[...] = w_ref[...].astype(BF16)

    o_ref[...] = jnp.dot(x_ref[...], wb_ref[...], preferred_element_type=F32).astype(o_ref.dtype)


def matmul(x, w, out_dtype, tm=512, tn=1024):
    m, k = x.shape
    n = w.shape[1]
    tm, tn = _fit(tm, m), _fit(tn, n)
    return pl.pallas_call(
        _mm_body,
        grid=(n // tn, m // tm),
        in_specs=[pl.BlockSpec((tm, k), lambda j, i: (i, 0)),
                  pl.BlockSpec((k, tn), lambda j, i: (0, j))],
        out_specs=pl.BlockSpec((tm, tn), lambda j, i: (i, j)),
        out_shape=jax.ShapeDtypeStruct((m, n), out_dtype),
        scratch_shapes=[pltpu.VMEM((k, tn), BF16)],
        compiler_params=_params(("parallel", "arbitrary"), V7X_VMEM_LIMIT),
        name="in_proj",
    )(x, w)


def _proj_ln_body(a_ref, w_ref, res_ref, g_ref, b_ref, of_ref, ob_ref, *, alpha, nk, ln_rows):
    k = pl.program_id(1)

    @pl.when(k == 0)
    def _():
        of_ref[...] = jnp.dot(a_ref[...], w_ref[...], preferred_element_type=F32)

    @pl.when(k > 0)
    def _():
        of_ref[...] += jnp.dot(a_ref[...], w_ref[...], preferred_element_type=F32)

    @pl.when(k == nk - 1)
    def _():
        def chunk(c, carry):
            rows = pl.ds(pl.multiple_of(c * ln_rows, ln_rows), ln_rows)
            y = _layer_norm(alpha * res_ref[rows, :] + of_ref[rows, :], g_ref[...], b_ref[...])
            of_ref[rows, :] = y
            ob_ref[rows, :] = y.astype(BF16)
            return carry
        lax.fori_loop(0, of_ref.shape[0] // ln_rows, chunk, 0)


def proj_res_ln(a, w, res, g, b, alpha, tm=512, tk=512):
    m, kdim = a.shape
    d = w.shape[1]
    tm, tk = _fit(tm, m), _fit(tk, kdim)
    nk = kdim // tk
    return pl.pallas_call(
        functools.partial(_proj_ln_body, alpha=alpha, nk=nk, ln_rows=min(64, tm)),
        grid=(m // tm, nk),
        in_specs=[pl.BlockSpec((tm, tk), lambda i, k: (i, k)),
                  pl.BlockSpec((tk, d), lambda i, k: (k, 0)),
                  pl.BlockSpec((tm, d), lambda i, k: (i, 0)),
                  pl.BlockSpec((1, d), lambda i, k: (0, 0)),
                  pl.BlockSpec((1, d), lambda i, k: (0, 0))],
        out_specs=[pl.BlockSpec((tm, d), lambda i, k: (i, 0)),
                   pl.BlockSpec((tm, d), lambda i, k: (i, 0))],
        out_shape=[jax.ShapeDtypeStruct((m, d), F32), jax.ShapeDtypeStruct((m, d), BF16)],
        compiler_params=_params(("parallel", "arbitrary"), V7X_VMEM_LIMIT),
        name="out_proj_ln",
    )(a, w, res, g.reshape(1, d), b.reshape(1, d))


Q_SCALE = 2.0 ** math.floor(math.log2(HEAD_DIM ** -0.5))
SCORE_MUL = HEAD_DIM ** -0.5 / Q_SCALE * LOG2E


def _exp2_scores(q, k, bias2):
    t = _nt_dot(q * Q_SCALE, k) * SCORE_MUL + bias2
    return jnp.exp2(t - jnp.max(t, axis=-1, keepdims=True))


def _attend(q, k, v, bias2):
    e = _exp2_scores(q, k, bias2).astype(BF16)
    dv = v.shape[1]
    o = jnp.dot(e, jnp.concatenate([v, jnp.ones((v.shape[0], HEAD_DIM), v.dtype)], axis=-1),
                preferred_element_type=F32)
    return o[:, :dv] * (1.0 / o[:, dv:dv + 1])


def _attend_wide(q, k, v, bias2):
    e = _exp2_scores(q, k, bias2)
    o = jnp.dot(e.astype(BF16), v, preferred_element_type=F32)
    return o * (1.0 / jnp.sum(e, axis=-1, keepdims=True))


def _band_bias(band_ref, i, kb0, kb1, nq):
    return jnp.concatenate([band_ref[nq - 1 - i + kb] for kb in range(kb0, kb1)], axis=-1)


def _fill_band(rel_ref, band_ref):
    nb, tq, _ = band_ref.shape

    @pl.when(pl.program_id(1) == 0)
    def _():
        for j in range(nb):
            window = jnp.broadcast_to(rel_ref[:, j * tq:(j + 2) * tq], (tq, 2 * tq))
            band_ref[j] = pltpu.roll(window, 1, 1, stride=1, stride_axis=0)[:, tq:]


def _diff_body(dl_ref, gain_ref, q_ref, k_ref, v_ref, rel_ref, o_ref, band_ref, *, tq, lam_init):
    _fill_band(rel_ref, band_ref)
    nq = k_ref.shape[0] // tq
    dl = dl_ref[...]
    lam = (jnp.exp(jnp.sum(dl[0:1] * dl[1:2], axis=-1, keepdims=True))
           - jnp.exp(jnp.sum(dl[2:3] * dl[3:4], axis=-1, keepdims=True)) + lam_init)
    for i in range(nq):
        bias2 = _band_bias(band_ref, i, 0, nq, nq)
        q = q_ref[i * tq:(i + 1) * tq, :]
        o0 = _attend_wide(q[:, :HEAD_DIM], k_ref[:, :HEAD_DIM], v_ref[...], bias2)
        o1 = _attend_wide(q[:, HEAD_DIM:], k_ref[:, HEAD_DIM:], v_ref[...], bias2)
        o = o0 - lam * o1
        o = o * lax.rsqrt(jnp.mean(o * o, axis=-1, keepdims=True) + LN_EPS) * gain_ref[...]
        o_ref[i * tq:(i + 1) * tq, :] = (o * (1.0 - lam_init)).astype(o_ref.dtype)


def _dil_body(q_ref, k_ref, v_ref, rel_ref, o_ref, band_ref, *, tq, reach):
    _fill_band(rel_ref, band_ref)
    nq = k_ref.shape[0] // tq
    for i in range(nq):
        live = [kb for kb in range(nq) if kb == i or (abs(kb - i) - 1) * tq + 1 <= reach]
        kb0, kb1 = live[0], live[-1] + 1
        o = _attend(q_ref[i * tq:(i + 1) * tq, :], k_ref[kb0 * tq:kb1 * tq, :],
                    v_ref[kb0 * tq:kb1 * tq, :], _band_bias(band_ref, i, kb0, kb1, nq))
        o_ref[i * tq:(i + 1) * tq, :] = o.astype(o_ref.dtype)


def _na_body(q_ref, k_ref, v_ref, bias_ref, o_ref, *, rows_q, win):
    rows = q_ref.shape[0] // GRID_W
    nrb = rows // rows_q
    tq = rows_q * GRID_W
    for rb in range(nrb):
        w0 = min(max(rb * rows_q - NA_KH // 2, 0), rows - win) * GRID_W
        case = 0 if rb == 0 else (2 if rb == nrb - 1 else 1)
        o = _attend(q_ref[rb * tq:(rb + 1) * tq, :], k_ref[w0:w0 + win * GRID_W, :],
                    v_ref[w0:w0 + win * GRID_W, :], bias_ref[case])
        o_ref[rb * tq:(rb + 1) * tq, :] = o.astype(o_ref.dtype)


def _toeplitz(u, t):
    lead = u.shape[:-1]
    span = 2 * t - 1
    m = jnp.tile(u, (1,) * len(lead) + (t,))[..., :t * span].reshape(lead + (t, span))
    return m[..., t - 1:]


def _t5_rel_table(table, s):
    rel = jnp.arange(-(s - 1), s, dtype=jnp.int32)
    half = T5_BUCKETS // 2
    exact = half // 2
    n = jnp.abs(rel)
    log_ratio = jnp.log(jnp.maximum(n, 1).astype(F32) / exact) / math.log(T5_MAX_DIST / exact)
    large = jnp.minimum(exact + (log_ratio * (half - exact)).astype(jnp.int32), half - 1)
    bucket = jnp.where(rel > 0, half, 0) + jnp.where(n < exact, n, large)
    return table[bucket].astype(F32)


def _dilated_reach():
    return max((window // 2 // dilation) * dilation for window, dilation in DIL_BRANCHES)


def _dilated_log_multiplicity(s):
    rel = jnp.arange(-(s - 1), s, dtype=jnp.int32)
    cnt = jnp.zeros(rel.shape, F32)
    for window, dilation in DIL_BRANCHES:
        reach = (window // 2 // dilation) * dilation
        cnt = cnt + ((rel % dilation == 0) & (jnp.abs(rel) <= reach)).astype(F32)
    return jnp.where(cnt > 0, jnp.log(jnp.maximum(cnt, 1.0)), NEG)


def even_attention(h, bsz, s, d, t5_table, diff_lambda, diff_gain, lam_init, tq=256):
    n = bsz * s
    dh = HEAD_DIM
    h_diff = d // (4 * dh)
    h_dil = d // (2 * dh)
    nq = s // tq
    rel_tab = _t5_rel_table(t5_table, s)

    def rel_rows(tab):
        return jnp.pad(tab.T * LOG2E, ((0, 0), (0, 1))).reshape(tab.shape[1], 1, 2 * s)

    rel_a = rel_rows(rel_tab[:, :h_diff])
    rel_b = rel_rows(rel_tab[:, h_diff:] + _dilated_log_multiplicity(s)[:, None])
    nb = 2 * nq - 1
    band_scratch = [pltpu.VMEM((nb, tq, tq), F32)]

    ya = pl.pallas_call(
        functools.partial(_diff_body, tq=tq, lam_init=lam_init),
        grid=(h_diff, bsz),
        in_specs=[pl.BlockSpec((4, dh), lambda hh, b: (0, 0)),
                  pl.BlockSpec((1, 2 * dh), lambda hh, b: (0, 0)),
                  pl.BlockSpec((s, 2 * dh), lambda hh, b: (b, hh)),
                  pl.BlockSpec((s, 2 * dh), lambda hh, b: (b, h_diff + hh)),
                  pl.BlockSpec((s, 2 * dh), lambda hh, b: (b, 2 * h_diff + hh)),
                  pl.BlockSpec((None, 1, 2 * s), lambda hh, b: (hh, 0, 0))],
        out_specs=pl.BlockSpec((s, 2 * dh), lambda hh, b: (b, hh)),
        out_shape=jax.ShapeDtypeStruct((n, h_diff * 2 * dh), BF16),
        scratch_shapes=band_scratch,
        compiler_params=_params(("parallel", "arbitrary"), V7X_VMEM_LIMIT),
        name="diff_attn",
    )(diff_lambda.astype(F32), diff_gain.astype(F32).reshape(1, 2 * dh), h, h, h, rel_a)

    c0 = 3 * h_diff * 2
    yb = pl.pallas_call(
        functools.partial(_dil_body, tq=tq, reach=_dilated_reach()),
        grid=(h_dil, bsz),
        in_specs=[pl.BlockSpec((s, dh), lambda hh, b: (b, c0 + hh)),
                  pl.BlockSpec((s, dh), lambda hh, b: (b, c0 + h_dil + hh)),
                  pl.BlockSpec((s, dh), lambda hh, b: (b, c0 + 2 * h_dil + hh)),
                  pl.BlockSpec((None, 1, 2 * s), lambda hh, b: (hh, 0, 0))],
        out_specs=pl.BlockSpec((s, dh), lambda hh, b: (b, hh)),
        out_shape=jax.ShapeDtypeStruct((n, h_dil * dh), BF16),
        scratch_shapes=band_scratch,
        compiler_params=_params(("parallel", "arbitrary"), V7X_VMEM_LIMIT),
        name="dilated_attn",
    )(h, h, h, rel_b)
    return jnp.concatenate([ya, yb], axis=-1)


def _na_bias_tiles(rpb, rows, rows_q, win):
    w = GRID_W
    kh = min(NA_KH, rows)
    col = jnp.arange(w, dtype=jnp.int32)
    cs = jnp.clip(col - NA_KW // 2, 0, w - NA_KW)
    col_mask = (col[None, :] >= cs[:, None]) & (col[None, :] < cs[:, None] + NA_KW)
    left = (w - 1) - (NA_KW - 1)
    u = jnp.pad(rpb.astype(F32), ((0, 0), (0, 0), (left, 2 * w - left - (2 * NA_KW - 1))),
                constant_values=NEG)
    cm = jnp.where(col_mask[None, None], _toeplitz(u, w), NEG)
    n_off = cm.shape[1]
    cm = jnp.concatenate([cm, jnp.full((rpb.shape[0], 1, w, w), NEG, F32)], axis=1)
    nrb = rows // rows_q
    block = []
    for rb in (0, 1, nrb - 1):
        r0 = rb * rows_q
        w0 = min(max(r0 - NA_KH // 2, 0), rows - win)
        block.append([])
        for qr in range(rows_q):
            r = r0 + qr
            rs = min(max(r - kh // 2, 0), rows - kh)
            block[-1].append([w0 + kj - r + NA_KH - 1 if rs <= w0 + kj < rs + kh else n_off
                              for kj in range(win)])
    tiles = cm.transpose(0, 2, 1, 3)[:, :, np.asarray(block, np.int32)]
    return tiles.transpose(0, 2, 3, 1, 4, 5).reshape(rpb.shape[0], 3, rows_q * w, win * w)


def odd_attention(h, bsz, s, d, rpb, rows_q=4, win=12):
    n = bsz * s
    dh = HEAD_DIM
    nh = d // dh
    rows = s // GRID_W
    nrb = rows // rows_q
    assert rows % rows_q == 0 and nrb >= 3 and rows >= win
    assert win >= rows_q + min(NA_KH, rows) - 1 and rows_q <= NA_KH // 2
    tq = rows_q * GRID_W
    bias = _na_bias_tiles(rpb.astype(F32) * LOG2E, rows, rows_q, win)
    return pl.pallas_call(
        functools.partial(_na_body, rows_q=rows_q, win=win),
        grid=(nh, bsz),
        in_specs=[pl.BlockSpec((s, dh), lambda hh, b: (b, hh)),
                  pl.BlockSpec((s, dh), lambda hh, b: (b, nh + hh)),
                  pl.BlockSpec((s, dh), lambda hh, b: (b, 2 * nh + hh)),
                  pl.BlockSpec((None, 3, tq, win * GRID_W), lambda hh, b: (hh, 0, 0, 0))],
        out_specs=pl.BlockSpec((s, dh), lambda hh, b: (b, hh)),
        out_shape=jax.ShapeDtypeStruct((n, d), BF16),
        compiler_params=_params(("parallel", "parallel")),
        name="na_attn",
    )(h, h, h, bias)


def _router_body(x_ref, wr_ref, rb_ref, tri_ref, eidx_ref, wts_ref, rank_ref, cnt_ref, carry_ref):
    ne, tm = wr_ref.shape[0], x_ref.shape[0]
    gsz = ne // N_GROUPS
    ninf = -jnp.inf

    @pl.when(pl.program_id(0) == 0)
    def _():
        carry_ref[...] = jnp.zeros_like(carry_ref)

    wr = wr_ref[...]
    x = x_ref[...]
    w_hi = wr.astype(BF16)
    w_lo = (wr - w_hi.astype(F32)).astype(BF16)
    x_hi = x.astype(BF16)
    x_lo = (x - x_hi.astype(F32)).astype(BF16)
    logits = _nt_dot(w_hi, x_hi) + (_nt_dot(w_hi, x_lo) + _nt_dot(w_lo, x_hi))
    scores = jax.nn.sigmoid(logits)
    sel = scores + rb_ref[...]

    g3 = sel.reshape(N_GROUPS, gsz, tm)
    i_in = lax.broadcasted_iota(jnp.int32, g3.shape, 1)
    m1 = jnp.max(g3, axis=1, keepdims=True)
    first = jnp.min(jnp.where(g3 == m1, i_in, gsz), axis=1, keepdims=True)
    m2 = jnp.max(jnp.where(i_in == first, ninf, g3), axis=1, keepdims=True)
    grp = m1 + m2

    i_g = lax.broadcasted_iota(jnp.int32, grp.shape, 0)
    keep = jnp.zeros(grp.shape, jnp.bool_)
    for _ in range(TOPK_GROUPS):
        gm = jnp.max(grp, axis=0, keepdims=True)
        gi = jnp.min(jnp.where(grp == gm, i_g, N_GROUPS), axis=0, keepdims=True)
        hit = i_g == gi
        keep = keep | hit
        grp = jnp.where(hit, ninf, grp)
    selm = jnp.where(keep, g3, ninf).reshape(ne, tm)

    i_e = lax.broadcasted_iota(jnp.int32, (ne, tm), 0)
    hits, eids, ws = [], [], []
    for _ in range(TOP_K):
        mx = jnp.max(selm, axis=0, keepdims=True)
        ei = jnp.min(jnp.where(selm == mx, i_e, ne), axis=0, keepdims=True)
        hit = i_e == ei
        ws.append(jnp.sum(jnp.where(hit, scores, 0.0), axis=0, keepdims=True))
        selm = jnp.where(hit, ninf, selm)
        hits.append(hit)
        eids.append(ei)
    wsum = ws[0]
    for wk in ws[1:]:
        wsum = wsum + wk

    msel = jnp.zeros((ne, tm), F32)
    for hit in hits:
        msel = msel + hit.astype(F32)
    incl = jnp.dot(msel.astype(BF16), tri_ref[...], preferred_element_type=F32)
    rank_ex = incl - msel + carry_ref[:, 0:1]
    ranks = [jnp.sum(jnp.where(hit, rank_ex, 0.0), axis=0, keepdims=True) for hit in hits]

    eidx_ref[...] = jnp.concatenate(eids, axis=0)
    wts_ref[...] = jnp.concatenate([wk / wsum * ROUTED_SCALE for wk in ws], axis=0)
    rank_ref[...] = jnp.concatenate(ranks, axis=0).astype(jnp.int32)
    carry_ref[...] = carry_ref[...] + jnp.sum(msel, axis=1, keepdims=True)
    cnt_ref[...] = carry_ref[...]


def router(x, w_router, router_bias, tm=512):
    n, d = x.shape
    ne = w_router.shape[1]
    tm = _fit(tm, n)
    tri = (jnp.arange(tm)[:, None] <= jnp.arange(tm)[None, :]).astype(BF16)
    kn = jax.ShapeDtypeStruct((TOP_K, n), jnp.int32)
    eidx, wts, rank, cnt = pl.pallas_call(
        _router_body,
        grid=(n // tm,),
        in_specs=[pl.BlockSpec((tm, d), lambda t: (t, 0)),
                  pl.BlockSpec((ne, d), lambda t: (0, 0)),
                  pl.BlockSpec((ne, 1), lambda t: (0, 0)),
                  pl.BlockSpec((tm, tm), lambda t: (0, 0))],
        out_specs=[pl.BlockSpec((TOP_K, tm), lambda t: (0, t)),
                   pl.BlockSpec((TOP_K, tm), lambda t: (0, t)),
                   pl.BlockSpec((TOP_K, tm), lambda t: (0, t)),
                   pl.BlockSpec((ne, 128), lambda t: (0, 0))],
        out_shape=[kn, jax.ShapeDtypeStruct((TOP_K, n), F32), kn,
                   jax.ShapeDtypeStruct((ne, 128), F32)],
        scratch_shapes=[pltpu.VMEM((ne, 128), F32)],
        compiler_params=_params(("arbitrary",)),
        name="router",
    )(x, w_router.T.astype(F32), router_bias.astype(F32).reshape(ne, 1), tri)
    return eidx, wts, rank, cnt[:, 0].astype(jnp.int32)


def _shared_body(x_ref, xb_ref, w13_ref, w2_ref, o_ref, *, alpha):
    f = w2_ref.shape[0]
    a = jnp.dot(xb_ref[...], w13_ref[...], preferred_element_type=F32)
    hs = jax.nn.silu(a[:, :f]) * a[:, f:]
    o_ref[...] = alpha * x_ref[...] + jnp.dot(hs.astype(BF16), w2_ref[...], preferred_element_type=F32)


def shared_ffn_residual(x, xb, ws13, ws2, alpha, tm=256):
    n, d = x.shape
    f = ws2.shape[0]
    tm = _fit(tm, n)
    tok = lambda i: (i, 0)
    const = lambda i: (0, 0)
    return pl.pallas_call(
        functools.partial(_shared_body, alpha=alpha),
        grid=(n // tm,),
        in_specs=[pl.BlockSpec((tm, d), tok), pl.BlockSpec((tm, d), tok),
                  pl.BlockSpec((d, 2 * f), const), pl.BlockSpec((f, d), const)],
        out_specs=pl.BlockSpec((tm, d), tok),
        out_shape=jax.ShapeDtypeStruct((n, d), F32),
        compiler_params=_params(("parallel",), V7X_VMEM_LIMIT),
        name="shared_ffn",
    )(x, xb, ws13, ws2)


def _slab_pitch(d):
    assert d % (2 * LANES) == 0
    return d // (2 * LANES) + 1


def _pack_bf16_pair(lo, hi):
    lo_bits = lax.bitcast_convert_type(lo.astype(BF16).astype(F32), U32)
    hi_bits = lax.bitcast_convert_type(hi.astype(BF16).astype(F32), U32)
    return (lo_bits >> 16) | (hi_bits & U32(HIGH_HALF))


def _unpack_bf16_pair(words):
    lo = lax.bitcast_convert_type(words << 16, F32)
    hi = lax.bitcast_convert_type(words & U32(HIGH_HALF), F32)
    return lo, hi


def _slab_copy(src, src_start, dst, dst_start, sem, pitch):
    return pltpu.make_async_copy(src.at[pl.ds(src_start, pitch), :],
                                 dst.at[pl.ds(dst_start, pitch), :], sem)


def _slab_block(s, rows, pitch):
    return (pl.ds(s, rows, stride=pitch), slice(None))


def _dispatch_body(pos_ref, x_ref, xs_hbm, slab, sem):
    tm, d = x_ref.shape
    pitch = _slab_pitch(d)
    half = d // 2
    for s in range(pitch - 1):
        slab[_slab_block(s, tm, pitch)] = _pack_bf16_pair(
            x_ref[:, s * LANES:(s + 1) * LANES], x_ref[:, half + s * LANES:half + (s + 1) * LANES])
    slab[_slab_block(pitch - 1, tm, pitch)] = jnp.zeros((tm, LANES), slab.dtype)
    for k in range(TOP_K):
        def issue(rb, c, k=k):
            for u in range(DMA_UNROLL):
                r = rb * DMA_UNROLL + u
                _slab_copy(slab, r * pitch, xs_hbm, pos_ref[0, k * tm + r], sem, pitch).start()
            return c
        lax.fori_loop(0, tm // DMA_UNROLL, issue, 0)
    for k in range(TOP_K):
        pltpu.make_async_copy(slab, xs_hbm.at[pl.ds(0, tm * pitch), :], sem).wait()


def dispatch(x, pos_tiles):
    n, d = x.shape
    nt, tm = pos_tiles.shape[0], pos_tiles.shape[2] // TOP_K
    pitch = _slab_pitch(d)
    return pl.pallas_call(
        _dispatch_body,
        grid=(nt,),
        in_specs=[pl.BlockSpec((None, 1, TOP_K * tm), lambda i: (i, 0, 0), memory_space=pltpu.SMEM),
                  pl.BlockSpec((tm, d), lambda i: (i, 0))],
        out_specs=pl.BlockSpec(memory_space=pl.ANY),
        out_shape=jax.ShapeDtypeStruct((TOP_K * n * pitch, LANES), U32),
        scratch_shapes=[pltpu.VMEM((tm * pitch, LANES), U32), pltpu.SemaphoreType.DMA],
        compiler_params=_params(("arbitrary",)),
        name="dispatch",
    )(pos_tiles, x)


def _expert_body(tile_ref, exp_ref, lo_ref, hi_ref, first_ref, change_ref, next_ref,
                 xs_ref, w1_hbm, w3_hbm, w2_hbm, ys_ref,
                 st1_ref, st3_ref, st2_ref, w13_ref, w2_ref, sems, *, layer):
    it = pl.program_id(0)
    f, d = w2_ref.shape
    pitch = _slab_pitch(d)
    half = d // 2
    tm = xs_ref.shape[0] // pitch
    per_dot = EXPERT_OUT_CHUNK // LANES
    lo, hi = lo_ref[it], hi_ref[it]

    def staging_copies(e):
        return (pltpu.make_async_copy(w1_hbm.at[layer, e], st1_ref, sems.at[0]),
                pltpu.make_async_copy(w3_hbm.at[layer, e], st3_ref, sems.at[1]),
                pltpu.make_async_copy(w2_hbm.at[layer, e], st2_ref, sems.at[2]))

    @pl.when(it == 0)
    def _():
        for cp in staging_copies(exp_ref[0]):
            cp.start()

    @pl.when(change_ref[it] == 1)
    def _():
        for cp in staging_copies(exp_ref[it]):
            cp.wait()
        w13_ref[:, :f] = st1_ref[...].astype(BF16)
        w13_ref[:, f:] = st3_ref[...].astype(BF16)
        w2_ref[...] = st2_ref[...].astype(BF16)

        @pl.when(next_ref[it] >= 0)
        def _():
            for cp in staging_copies(next_ref[it]):
                cp.start()

    @pl.when(hi > lo)
    def _():
        pairs = [_unpack_bf16_pair(xs_ref[_slab_block(s, tm, pitch)]) for s in range(pitch - 1)]
        x = jnp.concatenate([p[0].astype(BF16) for p in pairs] + [p[1].astype(BF16) for p in pairs],
                            axis=-1)
        a = jnp.dot(x, w13_ref[...], preferred_element_type=F32)
        g = jax.nn.silu(a[:, :f]) * a[:, f:]
        row = tile_ref[it] * tm + lax.broadcasted_iota(jnp.int32, (tm, 1), 0)
        mine = (row >= lo) & (row < hi)
        g = jnp.where(mine, g, 0.0).astype(BF16)

        def out_words(c):
            c0 = c * EXPERT_OUT_CHUNK
            y_lo = jnp.dot(g, w2_ref[:, c0:c0 + EXPERT_OUT_CHUNK], preferred_element_type=F32)
            y_hi = jnp.dot(g, w2_ref[:, half + c0:half + c0 + EXPERT_OUT_CHUNK],
                           preferred_element_type=F32)
            return _pack_bf16_pair(y_lo, y_hi)

        @pl.when(first_ref[it] == 1)
        def _():
            for c in range(half // EXPERT_OUT_CHUNK):
                w = out_words(c)
                for j in range(per_dot):
                    ys_ref[_slab_block(c * per_dot + j, tm, pitch)] = w[:, j * LANES:(j + 1) * LANES]
            ys_ref[_slab_block(pitch - 1, tm, pitch)] = jnp.zeros((tm, LANES), U32)

        @pl.when(first_ref[it] == 0)
        def _():
            for c in range(half // EXPERT_OUT_CHUNK):
                w = out_words(c)
                for j in range(per_dot):
                    blk = _slab_block(c * per_dot + j, tm, pitch)
                    ys_ref[blk] = jnp.where(mine, w[:, j * LANES:(j + 1) * LANES], ys_ref[blk])


def expert_ffn(xs, w1, w3, w2, layer, items, tm):
    f, d = w2.shape[2:]
    pitch = _slab_pitch(d)
    assert (d // 2) % EXPERT_OUT_CHUNK == 0 and (tm * pitch) % 8 == 0
    n_items = items[0].shape[0]
    slab_tile = lambda i, t, *_: (t[i], 0)
    grid_spec = pltpu.PrefetchScalarGridSpec(
        num_scalar_prefetch=len(items),
        grid=(n_items,),
        in_specs=[pl.BlockSpec((tm * pitch, LANES), slab_tile),
                  pl.BlockSpec(memory_space=pl.ANY),
                  pl.BlockSpec(memory_space=pl.ANY),
                  pl.BlockSpec(memory_space=pl.ANY)],
        out_specs=pl.BlockSpec((tm * pitch, LANES), slab_tile),
        scratch_shapes=[pltpu.VMEM((d, f), F32), pltpu.VMEM((d, f), F32), pltpu.VMEM((f, d), F32),
                        pltpu.VMEM((d, 2 * f), BF16), pltpu.VMEM((f, d), BF16),
                        pltpu.SemaphoreType.DMA((3,))],
    )
    return pl.pallas_call(
        functools.partial(_expert_body, layer=layer),
        grid_spec=grid_spec,
        out_shape=jax.ShapeDtypeStruct(xs.shape, U32),
        compiler_params=_params(("arbitrary",), V7X_VMEM_LIMIT),
        name="expert_ffn",
    )(*items, xs, w1, w3, w2)


def _work_items(counts, p, tm):
    ne = counts.shape[0]
    nt = p // tm
    n_items = nt + ne - 1
    ends = jnp.cumsum(counts)
    starts = ends - counts
    t_first = starts // tm
    t_last = jnp.maximum(ends - 1, 0) // tm
    per = jnp.where(counts > 0, t_last - t_first + 1, 0)
    item_end = jnp.cumsum(per)
    item_start = item_end - per
    total = item_end[-1]
    i = jnp.arange(n_items, dtype=jnp.int32)
    ic = jnp.minimum(i, total - 1)
    e = jnp.sum((item_end[None, :] <= ic[:, None]).astype(jnp.int32), axis=1)
    onehot = (e[:, None] == jnp.arange(ne)[None, :]).astype(jnp.int32)
    pick = lambda a: jnp.sum(onehot * a[None, :], axis=1)
    tile = pick(t_first) + ic - pick(item_start)
    lo = jnp.maximum(pick(starts), tile * tm)
    hi = jnp.minimum(pick(ends), (tile + 1) * tm)
    live = i < total
    lo = jnp.where(live, lo, 0)
    hi = jnp.where(live, hi, 0)
    first = (live & (lo == tile * tm)).astype(jnp.int32)
    change = jnp.concatenate([jnp.ones((1,), jnp.int32), (e[1:] != e[:-1]).astype(jnp.int32)])
    ex = jnp.arange(ne)
    later = (ex[None, :] > ex[:, None]) & (per > 0)[None, :]
    next_of = jnp.min(jnp.where(later, ex[None, :], ne), axis=1)
    nxt = pick(jnp.where(next_of < ne, next_of, -1))
    as_i32 = lambda a: a.astype(jnp.int32)
    return as_i32(tile), as_i32(e), as_i32(lo), as_i32(hi), first, change, as_i32(nxt)


def _combine_body(pos_ref, base_ref, wt_ref, g_ref, b_ref, ys_hbm, of_ref, ob_ref, *scratch):
    bufs, acc_ref, sem = scratch[:TOP_K], scratch[TOP_K], scratch[TOP_K + 1]
    tm, d = base_ref.shape
    pitch = _slab_pitch(d)
    half = d // 2

    for k in range(TOP_K):
        def issue(rb, c, k=k):
            for u in range(DMA_UNROLL):
                r = rb * DMA_UNROLL + u
                _slab_copy(ys_hbm, pos_ref[0, k * tm + r], bufs[k], r * pitch, sem, pitch).start()
            return c
        lax.fori_loop(0, tm // DMA_UNROLL, issue, 0)
    for k in range(TOP_K):
        pltpu.make_async_copy(ys_hbm.at[pl.ds(0, tm * pitch), :], bufs[k], sem).wait()

    for r0 in range(0, tm, COMBINE_ROWS):
        rows = slice(r0, r0 + COMBINE_ROWS)
        gates = [jnp.broadcast_to(wt_ref[rows, k:k + 1], (COMBINE_ROWS, LANES)) for k in range(TOP_K)]
        for s in range(pitch - 1):
            lo_cols = slice(s * LANES, (s + 1) * LANES)
            hi_cols = slice(half + s * LANES, half + (s + 1) * LANES)
            lo_acc = base_ref[rows, lo_cols]
            hi_acc = base_ref[rows, hi_cols]
            for k in range(TOP_K):
                lo, hi = _unpack_bf16_pair(bufs[k][pl.ds(r0 * pitch + s, COMBINE_ROWS, stride=pitch), :])
                lo_acc = lo_acc + gates[k] * lo
                hi_acc = hi_acc + gates[k] * hi
            acc_ref[rows, lo_cols] = lo_acc
            acc_ref[rows, hi_cols] = hi_acc
    y = _layer_norm(acc_ref[...], g_ref[...], b_ref[...])
    of_ref[...] = y
    ob_ref[...] = y.astype(BF16)


def combine_ln(base, ys, pos_tiles, wts_t, g, b):
    n, d = base.shape
    nt, tm = pos_tiles.shape[0], pos_tiles.shape[2] // TOP_K
    pitch = _slab_pitch(d)
    tok = lambda i: (i, 0)
    const = lambda i: (0, 0)
    return pl.pallas_call(
        _combine_body,
        grid=(nt,),
        in_specs=[pl.BlockSpec((None, 1, TOP_K * tm), lambda i: (i, 0, 0), memory_space=pltpu.SMEM),
                  pl.BlockSpec((tm, d), tok),
                  pl.BlockSpec((tm, TOP_K), tok),
                  pl.BlockSpec((1, d), const),
                  pl.BlockSpec((1, d), const),
                  pl.BlockSpec(memory_space=pl.ANY)],
        out_specs=[pl.BlockSpec((tm, d), tok), pl.BlockSpec((tm, d), tok)],
        out_shape=[jax.ShapeDtypeStruct((n, d), F32), jax.ShapeDtypeStruct((n, d), BF16)],
        scratch_shapes=[pltpu.VMEM((tm * pitch, LANES), U32) for _ in range(TOP_K)]
        + [pltpu.VMEM((tm, d), F32), pltpu.SemaphoreType.DMA],
        compiler_params=_params(("arbitrary",), V7X_VMEM_LIMIT),
        name="combine_ln",
    )(pos_tiles, base, wts_t, g.reshape(1, d), b.reshape(1, d), ys)


def moe_block(x, xb, w_router, router_bias, w1, w3, w2, layer, ws1, ws3, ws2, g, b, alpha,
              tm_tok=256, tm_exp=256):
    n, d = x.shape
    ne = w_router.shape[1]
    tm_tok, tm_exp = _fit(tm_tok, n), _fit(tm_exp, n)
    eidx, wts, rank, counts = router(x, w_router, router_bias)
    starts = jnp.cumsum(counts) - counts
    pos = rank + jnp.sum(jnp.where(eidx[..., None] == jnp.arange(ne), starts, 0), axis=-1)
    pos_tiles = (pos * _slab_pitch(d)).reshape(TOP_K, n // tm_tok, tm_tok).transpose(1, 0, 2)
    pos_tiles = pos_tiles.reshape(n // tm_tok, 1, TOP_K * tm_tok)
    xs = dispatch(x, pos_tiles)
    items = _work_items(counts, TOP_K * n, tm_exp)
    ys = expert_ffn(xs, w1, w3, w2, layer, items, tm_exp)
    ws13 = jnp.concatenate([ws1.astype(BF16), ws3.astype(BF16)], axis=-1)
    base = shared_ffn_residual(x, xb, ws13, ws2.astype(BF16), alpha)
    return combine_ln(base, ys, pos_tiles, wts.T, g, b)


def kernel(x, t5_table, w_in_even, diff_lambda, diff_gain, w_out_even, w_in_odd, na_rpb, w_out_odd,
           ln_mix_g, ln_mix_b, router_w, router_bias, exp_w1, exp_w3, exp_w2, sh_w1, sh_w3, sh_w2,
           ln_ffn_g, ln_ffn_b):
    bsz, s, d = x.shape
    depth = ln_mix_g.shape[0]
    alpha = (2 * depth) ** 0.25
    xf = x.reshape(bsz * s, d).astype(F32)
    xb = xf.astype(BF16)
    for l in range(depth):
        j = l // 2
        if l % 2 == 0:
            h = matmul(xb, w_in_even[j], BF16)
            lam_init = 0.8 - 0.6 * math.exp(-0.3 * l)
            att = even_attention(h, bsz, s, d, t5_table, diff_lambda[j], diff_gain[j], lam_init)
            w_out = w_out_even[j]
        else:
            h = matmul(xb, w_in_odd[j], BF16)
            att = odd_attention(h, bsz, s, d, na_rpb[j])
            w_out = w_out_odd[j]
        xf, xb = proj_res_ln(att, w_out.astype(BF16), xf, ln_mix_g[l], ln_mix_b[l], alpha)
        xf, xb = moe_block(xf, xb, router_w[l], router_bias[l], exp_w1, exp_w3, exp_w2, l,
                           sh_w1[l], sh_w3[l], sh_w2[l], ln_ffn_g[l], ln_ffn_b[l], alpha)
    return xf.reshape(bsz, s, d).astype(x.dtype)
```

```python
import functools
import math

import jax
import jax.numpy as jnp
import numpy as np
from jax import lax
from jax.experimental import pallas as pl
from jax.experimental.pallas import tpu as pltpu

HEAD_DIM = 128
DIL_BRANCHES = ((128, 1), (512, 4), (2048, 16))
GRID_W = 64
NA_KH = 8
NA_KW = 16
T5_BUCKETS = 32
T5_MAX_DIST = 1024
N_GROUPS = 8
TOPK_GROUPS = 4
TOP_K = 8
ROUTED_SCALE = 2.5
LN_EPS = 1e-5
NEG = -1e30
LOG2E = math.log2(math.e)

V7X_VMEM_LIMIT = 56 * 1024 * 1024
LANES = 128
EXPERT_OUT_CHUNK = 512
DMA_UNROLL = 8
COMBINE_ROWS = 32

F32 = jnp.float32
BF16 = jnp.bfloat16
U32 = jnp.uint32
HIGH_HALF = 0xFFFF0000


def _params(sem, vmem=None):
    return pltpu.CompilerParams(dimension_semantics=sem, vmem_limit_bytes=vmem)


def _fit(tile, dim):
    if dim <= tile:
        return dim
    return max(t for t in range(128, tile + 1, 128) if dim % t == 0)


def _nt_dot(a, b, **kw):
    return lax.dot_general(a, b, (((1,), (1,)), ((), ())), preferred_element_type=F32, **kw)


def _layer_norm(z, g, b):
    mu = jnp.mean(z, axis=-1, keepdims=True)
    zc = z - mu
    var = jnp.mean(zc * zc, axis=-1, keepdims=True)
    return zc * lax.rsqrt(var + LN_EPS) * g + b


def _mm_body(x_ref, w_ref, o_ref, wb_ref):
    @pl.when(pl.program_id(1) == 0)
    def _():
        wb_ref[...] = w_ref[...].astype(BF16)

    o_ref[...] = jnp.dot(x_ref[...], wb_ref[...], preferred_element_type=F32).astype(o_ref.dtype)


def matmul(x, w, out_dtype, tm=512, tn=1024):
    m, k = x.shape
    n = w.shape[1]
    tm, tn = _fit(tm, m), _fit(tn, n)
    return pl.pallas_call(
        _mm_body,
        grid=(n // tn, m // tm),
        in_specs=[pl.BlockSpec((tm, k), lambda j, i: (i, 0)),
                  pl.BlockSpec((k, tn), lambda j, i: (0, j))],
        out_specs=pl.BlockSpec((tm, tn), lambda j, i: (i, j)),
        out_shape=jax.ShapeDtypeStruct((m, n), out_dtype),
        scratch_shapes=[pltpu.VMEM((k, tn), BF16)],
        compiler_params=_params(("parallel", "arbitrary"), V7X_VMEM_LIMIT),
        name="in_proj",
    )(x, w)


def _proj_ln_body(a_ref, w_ref, res_ref, g_ref, b_ref, of_ref, ob_ref, *, alpha, nk, ln_rows):
    k = pl.program_id(1)

    @pl.when(k == 0)
    def _():
        of_ref[...] = jnp.dot(a_ref[...], w_ref[...], preferred_element_type=F32)

    @pl.when(k > 0)
    def _():
        of_ref[...] += jnp.dot(a_ref[...], w_ref[...], preferred_element_type=F32)

    @pl.when(k == nk - 1)
    def _():
        def chunk(c, carry):
            rows = pl.ds(pl.multiple_of(c * ln_rows, ln_rows), ln_rows)
            y = _layer_norm(alpha * res_ref[rows, :] + of_ref[rows, :], g_ref[...], b_ref[...])
            of_ref[rows, :] = y
            ob_ref[rows, :] = y.astype(BF16)
            return carry
        lax.fori_loop(0, of_ref.shape[0] // ln_rows, chunk, 0)


def proj_res_ln(a, w, res, g, b, alpha, tm=512, tk=512):
    m, kdim = a.shape
    d = w.shape[1]
    tm, tk = _fit(tm, m), _fit(tk, kdim)
    nk = kdim // tk
    return pl.pallas_call(
        functools.partial(_proj_ln_body, alpha=alpha, nk=nk, ln_rows=min(64, tm)),
        grid=(m // tm, nk),
        in_specs=[pl.BlockSpec((tm, tk), lambda i, k: (i, k)),
                  pl.BlockSpec((tk, d), lambda i, k: (k, 0)),
                  pl.BlockSpec((tm, d), lambda i, k: (i, 0)),
                  pl.BlockSpec((1, d), lambda i, k: (0, 0)),
                  pl.BlockSpec((1, d), lambda i, k: (0, 0))],
        out_specs=[pl.BlockSpec((tm, d), lambda i, k: (i, 0)),
                   pl.BlockSpec((tm, d), lambda i, k: (i, 0))],
        out_shape=[jax.ShapeDtypeStruct((m, d), F32), jax.ShapeDtypeStruct((m, d), BF16)],
        compiler_params=_params(("parallel", "arbitrary"), V7X_VMEM_LIMIT),
        name="out_proj_ln",
    )(a, w, res, g.reshape(1, d), b.reshape(1, d))


Q_SCALE = 2.0 ** math.floor(math.log2(HEAD_DIM ** -0.5))
SCORE_MUL = HEAD_DIM ** -0.5 / Q_SCALE * LOG2E


def _exp2_scores(q, k, bias2):
    t = _nt_dot(q * Q_SCALE, k) * SCORE_MUL + bias2
    return jnp.exp2(t - jnp.max(t, axis=-1, keepdims=True))


def _attend(q, k, v, bias2):
    e = _exp2_scores(q, k, bias2).astype(BF16)
    dv = v.shape[1]
    o = jnp.dot(e, jnp.concatenate([v, jnp.ones((v.shape[0], HEAD_DIM), v.dtype)], axis=-1),
                preferred_element_type=F32)
    return o[:, :dv] * (1.0 / o[:, dv:dv + 1])


def _attend_wide(q, k, v, bias2):
    e = _exp2_scores(q, k, bias2)
    o = jnp.dot(e.astype(BF16), v, preferred_element_type=F32)
    return o * (1.0 / jnp.sum(e, axis=-1, keepdims=True))


def _band_bias(band_ref, i, kb0, kb1, nq):
    return jnp.concatenate([band_ref[nq - 1 - i + kb] for kb in range(kb0, kb1)], axis=-1)


def _fill_band(rel_ref, band_ref):
    nb, tq, _ = band_ref.shape

    @pl.when(pl.program_id(1) == 0)
    def _():
        for j in range(nb):
            window = jnp.broadcast_to(rel_ref[:, j * tq:(j + 2) * tq], (tq, 2 * tq))
            band_ref[j] = pltpu.roll(window, 1, 1, stride=1, stride_axis=0)[:, tq:]


def _diff_body(dl_ref, gain_ref, q_ref, k_ref, v_ref, rel_ref, o_ref, band_ref, *, tq, lam_init):
    _fill_band(rel_ref, band_ref)
    nq = k_ref.shape[0] // tq
    dl = dl_ref[...]
    lam = (jnp.exp(jnp.sum(dl[0:1] * dl[1:2], axis=-1, keepdims=True))
           - jnp.exp(jnp.sum(dl[2:3] * dl[3:4], axis=-1, keepdims=True)) + lam_init)
    for i in range(nq):
        bias2 = _band_bias(band_ref, i, 0, nq, nq)
        q = q_ref[i * tq:(i + 1) * tq, :]
        o0 = _attend_wide(q[:, :HEAD_DIM], k_ref[:, :HEAD_DIM], v_ref[...], bias2)
        o1 = _attend_wide(q[:, HEAD_DIM:], k_ref[:, HEAD_DIM:], v_ref[...], bias2)
        o = o0 - lam * o1
        o = o * lax.rsqrt(jnp.mean(o * o, axis=-1, keepdims=True) + LN_EPS) * gain_ref[...]
        o_ref[i * tq:(i + 1) * tq, :] = (o * (1.0 - lam_init)).astype(o_ref.dtype)


def _dil_body(q_ref, k_ref, v_ref, rel_ref, o_ref, band_ref, *, tq, reach):
    _fill_band(rel_ref, band_ref)
    nq = k_ref.shape[0] // tq
    for i in range(nq):
        live = [kb for kb in range(nq) if kb == i or (abs(kb - i) - 1) * tq + 1 <= reach]
        kb0, kb1 = live[0], live[-1] + 1
        o = _attend(q_ref[i * tq:(i + 1) * tq, :], k_ref[kb0 * tq:kb1 * tq, :],
                    v_ref[kb0 * tq:kb1 * tq, :], _band_bias(band_ref, i, kb0, kb1, nq))
        o_ref[i * tq:(i + 1) * tq, :] = o.astype(o_ref.dtype)


def _na_body(q_ref, k_ref, v_ref, bias_ref, o_ref, *, rows_q, win):
    rows = q_ref.shape[0] // GRID_W
    nrb = rows // rows_q
    tq = rows_q * GRID_W
    for rb in range(nrb):
        w0 = min(max(rb * rows_q - NA_KH // 2, 0), rows - win) * GRID_W
        case = 0 if rb == 0 else (2 if rb == nrb - 1 else 1)
        o = _attend(q_ref[rb * tq:(rb + 1) * tq, :], k_ref[w0:w0 + win * GRID_W, :],
                    v_ref[w0:w0 + win * GRID_W, :], bias_ref[case])
        o_ref[rb * tq:(rb + 1) * tq, :] = o.astype(o_ref.dtype)


def _toeplitz(u, t):
    lead = u.shape[:-1]
    span = 2 * t - 1
    m = jnp.tile(u, (1,) * len(lead) + (t,))[..., :t * span].reshape(lead + (t, span))
    return m[..., t - 1:]


def _t5_rel_table(table, s):
    rel = jnp.arange(-(s - 1), s, dtype=jnp.int32)
    half = T5_BUCKETS // 2
    exact = half // 2
    n = jnp.abs(rel)
    log_ratio = jnp.log(jnp.maximum(n, 1).astype(F32) / exact) / math.log(T5_MAX_DIST / exact)
    large = jnp.minimum(exact + (log_ratio * (half - exact)).astype(jnp.int32), half - 1)
    bucket = jnp.where(rel > 0, half, 0) + jnp.where(n < exact, n, large)
    return table[bucket].astype(F32)


def _dilated_reach():
    return max((window // 2 // dilation) * dilation for window, dilation in DIL_BRANCHES)


def _dilated_log_multiplicity(s):
    rel = jnp.arange(-(s - 1), s, dtype=jnp.int32)
    cnt = jnp.zeros(rel.shape, F32)
    for window, dilation in DIL_BRANCHES:
        reach = (window // 2 // dilation) * dilation
        cnt = cnt + ((rel % dilation == 0) & (jnp.abs(rel) <= reach)).astype(F32)
    return jnp.where(cnt > 0, jnp.log(jnp.maximum(cnt, 1.0)), NEG)


def even_attention(h, bsz, s, d, t5_table, diff_lambda, diff_gain, lam_init, tq=256):
    n = bsz * s
    dh = HEAD_DIM
    h_diff = d // (4 * dh)
    h_dil = d // (2 * dh)
    nq = s // tq
    rel_tab = _t5_rel_table(t5_table, s)

    def rel_rows(tab):
        return jnp.pad(tab.T * LOG2E, ((0, 0), (0, 1))).reshape(tab.shape[1], 1, 2 * s)

    rel_a = rel_rows(rel_tab[:, :h_diff])
    rel_b = rel_rows(rel_tab[:, h_diff:] + _dilated_log_multiplicity(s)[:, None])
    nb = 2 * nq - 1
    band_scratch = [pltpu.VMEM((nb, tq, tq), F32)]

    ya = pl.pallas_call(
        functools.partial(_diff_body, tq=tq, lam_init=lam_init),
        grid=(h_diff, bsz),
        in_specs=[pl.BlockSpec((4, dh), lambda hh, b: (0, 0)),
                  pl.BlockSpec((1, 2 * dh), lambda hh, b: (0, 0)),
                  pl.BlockSpec((s, 2 * dh), lambda hh, b: (b, hh)),
                  pl.BlockSpec((s, 2 * dh), lambda hh, b: (b, h_diff + hh)),
                  pl.BlockSpec((s, 2 * dh), lambda hh, b: (b, 2 * h_diff + hh)),
                  pl.BlockSpec((None, 1, 2 * s), lambda hh, b: (hh, 0, 0))],
        out_specs=pl.BlockSpec((s, 2 * dh), lambda hh, b: (b, hh)),
        out_shape=jax.ShapeDtypeStruct((n, h_diff * 2 * dh), BF16),
        scratch_shapes=band_scratch,
        compiler_params=_params(("parallel", "arbitrary"), V7X_VMEM_LIMIT),
        name="diff_attn",
    )(diff_lambda.astype(F32), diff_gain.astype(F32).reshape(1, 2 * dh), h, h, h, rel_a)

    c0 = 3 * h_diff * 2
    yb = pl.pallas_call(
        functools.partial(_dil_body, tq=tq, reach=_dilated_reach()),
        grid=(h_dil, bsz),
        in_specs=[pl.BlockSpec((s, dh), lambda hh, b: (b, c0 + hh)),
                  pl.BlockSpec((s, dh), lambda hh, b: (b, c0 + h_dil + hh)),
                  pl.BlockSpec((s, dh), lambda hh, b: (b, c0 + 2 * h_dil + hh)),
                  pl.BlockSpec((None, 1, 2 * s), lambda hh, b: (hh, 0, 0))],
        out_specs=pl.BlockSpec((s, dh), lambda hh, b: (b, hh)),
        out_shape=jax.ShapeDtypeStruct((n, h_dil * dh), BF16),
        scratch_shapes=band_scratch,
        compiler_params=_params(("parallel", "arbitrary"), V7X_VMEM_LIMIT),
        name="dilated_attn",
    )(h, h, h, rel_b)
    return jnp.concatenate([ya, yb], axis=-1)


def _na_bias_tiles(rpb, rows, rows_q, win):
    w = GRID_W
    kh = min(NA_KH, rows)
    col = jnp.arange(w, dtype=jnp.int32)
    cs = jnp.clip(col - NA_KW // 2, 0, w - NA_KW)
    col_mask = (col[None, :] >= cs[:, None]) & (col[None, :] < cs[:, None] + NA_KW)
    left = (w - 1) - (NA_KW - 1)
    u = jnp.pad(rpb.astype(F32), ((0, 0), (0, 0), (left, 2 * w - left - (2 * NA_KW - 1))),
                constant_values=NEG)
    cm = jnp.where(col_mask[None, None], _toeplitz(u, w), NEG)
    n_off = cm.shape[1]
    cm = jnp.concatenate([cm, jnp.full((rpb.shape[0], 1, w, w), NEG, F32)], axis=1)
    nrb = rows // rows_q
    block = []
    for rb in (0, 1, nrb - 1):
        r0 = rb * rows_q
        w0 = min(max(r0 - NA_KH // 2, 0), rows - win)
        block.append([])
        for qr in range(rows_q):
            r = r0 + qr
            rs = min(max(r - kh // 2, 0), rows - kh)
            block[-1].append([w0 + kj - r + NA_KH - 1 if rs <= w0 + kj < rs + kh else n_off
                              for kj in range(win)])
    tiles = cm.transpose(0, 2, 1, 3)[:, :, np.asarray(block, np.int32)]
    return tiles.transpose(0, 2, 3, 1, 4, 5).reshape(rpb.shape[0], 3, rows_q * w, win * w)


def odd_attention(h, bsz, s, d, rpb, rows_q=4, win=12):
    n = bsz * s
    dh = HEAD_DIM
    nh = d // dh
    rows = s // GRID_W
    nrb = rows // rows_q
    assert rows % rows_q == 0 and nrb >= 3 and rows >= win
    assert win >= rows_q + min(NA_KH, rows) - 1 and rows_q <= NA_KH // 2
    tq = rows_q * GRID_W
    bias = _na_bias_tiles(rpb.astype(F32) * LOG2E, rows, rows_q, win)
    return pl.pallas_call(
        functools.partial(_na_body, rows_q=rows_q, win=win),
        grid=(nh, bsz),
        in_specs=[pl.BlockSpec((s, dh), lambda hh, b: (b, hh)),
                  pl.BlockSpec((s, dh), lambda hh, b: (b, nh + hh)),
                  pl.BlockSpec((s, dh), lambda hh, b: (b, 2 * nh + hh)),
                  pl.BlockSpec((None, 3, tq, win * GRID_W), lambda hh, b: (hh, 0, 0, 0))],
        out_specs=pl.BlockSpec((s, dh), lambda hh, b: (b, hh)),
        out_shape=jax.ShapeDtypeStruct((n, d), BF16),
        compiler_params=_params(("parallel", "parallel")),
        name="na_attn",
    )(h, h, h, bias)


def _router_body(x_ref, wr_ref, rb_ref, tri_ref, eidx_ref, wts_ref, rank_ref, cnt_ref, carry_ref):
    ne, tm = wr_ref.shape[0], x_ref.shape[0]
    gsz = ne // N_GROUPS
    ninf = -jnp.inf

    @pl.when(pl.program_id(0) == 0)
    def _():
        carry_ref[...] = jnp.zeros_like(carry_ref)

    wr = wr_ref[...]
    x = x_ref[...]
    w_hi = wr.astype(BF16)
    w_lo = (wr - w_hi.astype(F32)).astype(BF16)
    x_hi = x.astype(BF16)
    x_lo = (x - x_hi.astype(F32)).astype(BF16)
    logits = _nt_dot(w_hi, x_hi) + (_nt_dot(w_hi, x_lo) + _nt_dot(w_lo, x_hi))
    scores = jax.nn.sigmoid(logits)
    sel = scores + rb_ref[...]

    g3 = sel.reshape(N_GROUPS, gsz, tm)
    i_in = lax.broadcasted_iota(jnp.int32, g3.shape, 1)
    m1 = jnp.max(g3, axis=1, keepdims=True)
    first = jnp.min(jnp.where(g3 == m1, i_in, gsz), axis=1, keepdims=True)
    m2 = jnp.max(jnp.where(i_in == first, ninf, g3), axis=1, keepdims=True)
    grp = m1 + m2

    i_g = lax.broadcasted_iota(jnp.int32, grp.shape, 0)
    keep = jnp.zeros(grp.shape, jnp.bool_)
    for _ in range(TOPK_GROUPS):
        gm = jnp.max(grp, axis=0, keepdims=True)
        gi = jnp.min(jnp.where(grp == gm, i_g, N_GROUPS), axis=0, keepdims=True)
        hit = i_g == gi
        keep = keep | hit
        grp = jnp.where(hit, ninf, grp)
    selm = jnp.where(keep, g3, ninf).reshape(ne, tm)

    i_e = lax.broadcasted_iota(jnp.int32, (ne, tm), 0)
    hits, eids, ws = [], [], []
    for _ in range(TOP_K):
        mx = jnp.max(selm, axis=0, keepdims=True)
        ei = jnp.min(jnp.where(selm == mx, i_e, ne), axis=0, keepdims=True)
        hit = i_e == ei
        ws.append(jnp.sum(jnp.where(hit, scores, 0.0), axis=0, keepdims=True))
        selm = jnp.where(hit, ninf, selm)
        hits.append(hit)
        eids.append(ei)
    wsum = ws[0]
    for wk in ws[1:]:
        wsum = wsum + wk

    msel = jnp.zeros((ne, tm), F32)
    for hit in hits:
        msel = msel + hit.astype(F32)
    incl = jnp.dot(msel.astype(BF16), tri_ref[...], preferred_element_type=F32)
    rank_ex = incl - msel + carry_ref[:, 0:1]
    ranks = [jnp.sum(jnp.where(hit, rank_ex, 0.0), axis=0, keepdims=True) for hit in hits]

    eidx_ref[...] = jnp.concatenate(eids, axis=0)
    wts_ref[...] = jnp.concatenate([wk / wsum * ROUTED_SCALE for wk in ws], axis=0)
    rank_ref[...] = jnp.concatenate(ranks, axis=0).astype(jnp.int32)
    carry_ref[...] = carry_ref[...] + jnp.sum(msel, axis=1, keepdims=True)
    cnt_ref[...] = carry_ref[...]


def router(x, w_router, router_bias, tm=512):
    n, d = x.shape
    ne = w_router.shape[1]
    tm = _fit(tm, n)
    tri = (jnp.arange(tm)[:, None] <= jnp.arange(tm)[None, :]).astype(BF16)
    kn = jax.ShapeDtypeStruct((TOP_K, n), jnp.int32)
    eidx, wts, rank, cnt = pl.pallas_call(
        _router_body,
        grid=(n // tm,),
        in_specs=[pl.BlockSpec((tm, d), lambda t: (t, 0)),
                  pl.BlockSpec((ne, d), lambda t: (0, 0)),
                  pl.BlockSpec((ne, 1), lambda t: (0, 0)),
                  pl.BlockSpec((tm, tm), lambda t: (0, 0))],
        out_specs=[pl.BlockSpec((TOP_K, tm), lambda t: (0, t)),
                   pl.BlockSpec((TOP_K, tm), lambda t: (0, t)),
                   pl.BlockSpec((TOP_K, tm), lambda t: (0, t)),
                   pl.BlockSpec((ne, 128), lambda t: (0, 0))],
        out_shape=[kn, jax.ShapeDtypeStruct((TOP_K, n), F32), kn,
                   jax.ShapeDtypeStruct((ne, 128), F32)],
        scratch_shapes=[pltpu.VMEM((ne, 128), F32)],
        compiler_params=_params(("arbitrary",)),
        name="router",
    )(x, w_router.T.astype(F32), router_bias.astype(F32).reshape(ne, 1), tri)
    return eidx, wts, rank, cnt[:, 0].astype(jnp.int32)


def _shared_body(x_ref, xb_ref, w13_ref, w2_ref, o_ref, *, alpha):
    f = w2_ref.shape[0]
    a = jnp.dot(xb_ref[...], w13_ref[...], preferred_element_type=F32)
    hs = jax.nn.silu(a[:, :f]) * a[:, f:]
    o_ref[...] = alpha * x_ref[...] + jnp.dot(hs.astype(BF16), w2_ref[...], preferred_element_type=F32)


def shared_ffn_residual(x, xb, ws13, ws2, alpha, tm=256):
    n, d = x.shape
    f = ws2.shape[0]
    tm = _fit(tm, n)
    tok = lambda i: (i, 0)
    const = lambda i: (0, 0)
    return pl.pallas_call(
        functools.partial(_shared_body, alpha=alpha),
        grid=(n // tm,),
        in_specs=[pl.BlockSpec((tm, d), tok), pl.BlockSpec((tm, d), tok),
                  pl.BlockSpec((d, 2 * f), const), pl.BlockSpec((f, d), const)],
        out_specs=pl.BlockSpec((tm, d), tok),
        out_shape=jax.ShapeDtypeStruct((n, d), F32),
        compiler_params=_params(("parallel",), V7X_VMEM_LIMIT),
        name="shared_ffn",
    )(x, xb, ws13, ws2)


def _slab_pitch(d):
    assert d % (2 * LANES) == 0
    return d // (2 * LANES) + 1


def _pack_bf16_pair(lo, hi):
    lo_bits = lax.bitcast_convert_type(lo.astype(BF16).astype(F32), U32)
    hi_bits = lax.bitcast_convert_type(hi.astype(BF16).astype(F32), U32)
    return (lo_bits >> 16) | (hi_bits & U32(HIGH_HALF))


def _unpack_bf16_pair(words):
    lo = lax.bitcast_convert_type(words << 16, F32)
    hi = lax.bitcast_convert_type(words & U32(HIGH_HALF), F32)
    return lo, hi


def _slab_copy(src, src_start, dst, dst_start, sem, pitch):
    return pltpu.make_async_copy(src.at[pl.ds(src_start, pitch), :],
                                 dst.at[pl.ds(dst_start, pitch), :], sem)


def _slab_block(s, rows, pitch):
    return (pl.ds(s, rows, stride=pitch), slice(None))


def _dispatch_body(pos_ref, x_ref, xs_hbm, slab, sem):
    tm, d = x_ref.shape
    pitch = _slab_pitch(d)
    half = d // 2
    for s in range(pitch - 1):
        slab[_slab_block(s, tm, pitch)] = _pack_bf16_pair(
            x_ref[:, s * LANES:(s + 1) * LANES], x_ref[:, half + s * LANES:half + (s + 1) * LANES])
    slab[_slab_block(pitch - 1, tm, pitch)] = jnp.zeros((tm, LANES), slab.dtype)
    for k in range(TOP_K):
        def issue(rb, c, k=k):
            for u in range(DMA_UNROLL):
                r = rb * DMA_UNROLL + u
                _slab_copy(slab, r * pitch, xs_hbm, pos_ref[0, k * tm + r], sem, pitch).start()
            return c
        lax.fori_loop(0, tm // DMA_UNROLL, issue, 0)
    for k in range(TOP_K):
        pltpu.make_async_copy(slab, xs_hbm.at[pl.ds(0, tm * pitch), :], sem).wait()


def dispatch(x, pos_tiles):
    n, d = x.shape
    nt, tm = pos_tiles.shape[0], pos_tiles.shape[2] // TOP_K
    pitch = _slab_pitch(d)
    return pl.pallas_call(
        _dispatch_body,
        grid=(nt,),
        in_specs=[pl.BlockSpec((None, 1, TOP_K * tm), lambda i: (i, 0, 0), memory_space=pltpu.SMEM),
                  pl.BlockSpec((tm, d), lambda i: (i, 0))],
        out_specs=pl.BlockSpec(memory_space=pl.ANY),
        out_shape=jax.ShapeDtypeStruct((TOP_K * n * pitch, LANES), U32),
        scratch_shapes=[pltpu.VMEM((tm * pitch, LANES), U32), pltpu.SemaphoreType.DMA],
        compiler_params=_params(("arbitrary",)),
        name="dispatch",
    )(pos_tiles, x)


def _expert_body(tile_ref, exp_ref, lo_ref, hi_ref, first_ref, change_ref, next_ref,
                 xs_ref, w1_hbm, w3_hbm, w2_hbm, ys_ref,
                 st1_ref, st3_ref, st2_ref, w13_ref, w2_ref, sems, *, layer):
    it = pl.program_id(0)
    f, d = w2_ref.shape
    pitch = _slab_pitch(d)
    half = d // 2
    tm = xs_ref.shape[0] // pitch
    per_dot = EXPERT_OUT_CHUNK // LANES
    lo, hi = lo_ref[it], hi_ref[it]

    def staging_copies(e):
        return (pltpu.make_async_copy(w1_hbm.at[layer, e], st1_ref, sems.at[0]),
                pltpu.make_async_copy(w3_hbm.at[layer, e], st3_ref, sems.at[1]),
                pltpu.make_async_copy(w2_hbm.at[layer, e], st2_ref, sems.at[2]))

    @pl.when(it == 0)
    def _():
        for cp in staging_copies(exp_ref[0]):
            cp.start()

    @pl.when(change_ref[it] == 1)
    def _():
        for cp in staging_copies(exp_ref[it]):
            cp.wait()
        w13_ref[:, :f] = st1_ref[...].astype(BF16)
        w13_ref[:, f:] = st3_ref[...].astype(BF16)
        w2_ref[...] = st2_ref[...].astype(BF16)

        @pl.when(next_ref[it] >= 0)
        def _():
            for cp in staging_copies(next_ref[it]):
                cp.start()

    @pl.when(hi > lo)
    def _():
        pairs = [_unpack_bf16_pair(xs_ref[_slab_block(s, tm, pitch)]) for s in range(pitch - 1)]
        x = jnp.concatenate([p[0].astype(BF16) for p in pairs] + [p[1].astype(BF16) for p in pairs],
                            axis=-1)
        a = jnp.dot(x, w13_ref[...], preferred_element_type=F32)
        g = jax.nn.silu(a[:, :f]) * a[:, f:]
        row = tile_ref[it] * tm + lax.broadcasted_iota(jnp.int32, (tm, 1), 0)
        mine = (row >= lo) & (row < hi)
        g = jnp.where(mine, g, 0.0).astype(BF16)

        def out_words(c):
            c0 = c * EXPERT_OUT_CHUNK
            y_lo = jnp.dot(g, w2_ref[:, c0:c0 + EXPERT_OUT_CHUNK], preferred_element_type=F32)
            y_hi = jnp.dot(g, w2_ref[:, half + c0:half + c0 + EXPERT_OUT_CHUNK],
                           preferred_element_type=F32)
            return _pack_bf16_pair(y_lo, y_hi)

        @pl.when(first_ref[it] == 1)
        def _():
            for c in range(half // EXPERT_OUT_CHUNK):
                w = out_words(c)
                for j in range(per_dot):
                    ys_ref[_slab_block(c * per_dot + j, tm, pitch)] = w[:, j * LANES:(j + 1) * LANES]
            ys_ref[_slab_block(pitch - 1, tm, pitch)] = jnp.zeros((tm, LANES), U32)

        @pl.when(first_ref[it] == 0)
        def _():
            for c in range(half // EXPERT_OUT_CHUNK):
                w = out_words(c)
                for j in range(per_dot):
                    blk = _slab_block(c * per_dot + j, tm, pitch)
                    ys_ref[blk] = jnp.where(mine, w[:, j * LANES:(j + 1) * LANES], ys_ref[blk])


def expert_ffn(xs, w1, w3, w2, layer, items, tm):
    f, d = w2.shape[2:]
    pitch = _slab_pitch(d)
    assert (d // 2) % EXPERT_OUT_CHUNK == 0 and (tm * pitch) % 8 == 0
    n_items = items[0].shape[0]
    slab_tile = lambda i, t, *_: (t[i], 0)
    grid_spec = pltpu.PrefetchScalarGridSpec(
        num_scalar_prefetch=len(items),
        grid=(n_items,),
        in_specs=[pl.BlockSpec((tm * pitch, LANES), slab_tile),
                  pl.BlockSpec(memory_space=pl.ANY),
                  pl.BlockSpec(memory_space=pl.ANY),
                  pl.BlockSpec(memory_space=pl.ANY)],
        out_specs=pl.BlockSpec((tm * pitch, LANES), slab_tile),
        scratch_shapes=[pltpu.VMEM((d, f), F32), pltpu.VMEM((d, f), F32), pltpu.VMEM((f, d), F32),
                        pltpu.VMEM((d, 2 * f), BF16), pltpu.VMEM((f, d), BF16),
                        pltpu.SemaphoreType.DMA((3,))],
    )
    return pl.pallas_call(
        functools.partial(_expert_body, layer=layer),
        grid_spec=grid_spec,
        out_shape=jax.ShapeDtypeStruct(xs.shape, U32),
        compiler_params=_params(("arbitrary",), V7X_VMEM_LIMIT),
        name="expert_ffn",
    )(*items, xs, w1, w3, w2)


def _work_items(counts, p, tm):
    ne = counts.shape[0]
    nt = p // tm
    n_items = nt + ne - 1
    ends = jnp.cumsum(counts)
    starts = ends - counts
    t_first = starts // tm
    t_last = jnp.maximum(ends - 1, 0) // tm
    per = jnp.where(counts > 0, t_last - t_first + 1, 0)
    item_end = jnp.cumsum(per)
    item_start = item_end - per
    total = item_end[-1]
    i = jnp.arange(n_items, dtype=jnp.int32)
    ic = jnp.minimum(i, total - 1)
    e = jnp.sum((item_end[None, :] <= ic[:, None]).astype(jnp.int32), axis=1)
    onehot = (e[:, None] == jnp.arange(ne)[None, :]).astype(jnp.int32)
    pick = lambda a: jnp.sum(onehot * a[None, :], axis=1)
    tile = pick(t_first) + ic - pick(item_start)
    lo = jnp.maximum(pick(starts), tile * tm)
    hi = jnp.minimum(pick(ends), (tile + 1) * tm)
    live = i < total
    lo = jnp.where(live, lo, 0)
    hi = jnp.where(live, hi, 0)
    first = (live & (lo == tile * tm)).astype(jnp.int32)
    change = jnp.concatenate([jnp.ones((1,), jnp.int32), (e[1:] != e[:-1]).astype(jnp.int32)])
    ex = jnp.arange(ne)
    later = (ex[None, :] > ex[:, None]) & (per > 0)[None, :]
    next_of = jnp.min(jnp.where(later, ex[None, :], ne), axis=1)
    nxt = pick(jnp.where(next_of < ne, next_of, -1))
    as_i32 = lambda a: a.astype(jnp.int32)
    return as_i32(tile), as_i32(e), as_i32(lo), as_i32(hi), first, change, as_i32(nxt)


def _combine_body(pos_ref, nxt_ref, base_ref, wt_ref, g_ref, b_ref, ys_hbm, of_ref, ob_ref, *scratch):
    sets = (scratch[:TOP_K], scratch[TOP_K:2 * TOP_K])
    acc_ref, sems = scratch[2 * TOP_K], scratch[2 * TOP_K + 1]
    it = pl.program_id(0)
    tm, d = base_ref.shape
    pitch = _slab_pitch(d)
    half = d // 2

    def request(table_ref, slot):
        for k in range(TOP_K):
            def issue(rb, c, k=k):
                for u in range(DMA_UNROLL):
                    r = rb * DMA_UNROLL + u
                    _slab_copy(ys_hbm, table_ref[0, k * tm + r], sets[slot][k], r * pitch,
                               sems.at[slot], pitch).start()
                return c
            lax.fori_loop(0, tm // DMA_UNROLL, issue, 0)

    def accumulate(slot):
        bufs = sets[slot]
        for k in range(TOP_K):
            pltpu.make_async_copy(ys_hbm.at[pl.ds(0, tm * pitch), :], bufs[k], sems.at[slot]).wait()
        for r0 in range(0, tm, COMBINE_ROWS):
            rows = slice(r0, r0 + COMBINE_ROWS)
            gates = [jnp.broadcast_to(wt_ref[rows, k:k + 1], (COMBINE_ROWS, LANES)) for k in range(TOP_K)]
            for s in range(pitch - 1):
                lo_cols = slice(s * LANES, (s + 1) * LANES)
                hi_cols = slice(half + s * LANES, half + (s + 1) * LANES)
                lo_acc = base_ref[rows, lo_cols]
                hi_acc = base_ref[rows, hi_cols]
                for k in range(TOP_K):
                    lo, hi = _unpack_bf16_pair(
                        bufs[k][pl.ds(r0 * pitch + s, COMBINE_ROWS, stride=pitch), :])
                    lo_acc = lo_acc + gates[k] * lo
                    hi_acc = hi_acc + gates[k] * hi
                acc_ref[rows, lo_cols] = lo_acc
                acc_ref[rows, hi_cols] = hi_acc

    @pl.when(it == 0)
    def _():
        request(pos_ref, 0)

    for slot in (0, 1):
        @pl.when(it % 2 == slot)
        def _(slot=slot):
            @pl.when(it + 1 < pl.num_programs(0))
            def _():
                request(nxt_ref, 1 - slot)
            accumulate(slot)

    y = _layer_norm(acc_ref[...], g_ref[...], b_ref[...])
    of_ref[...] = y
    ob_ref[...] = y.astype(BF16)


def combine_ln(base, ys, pos_tiles, wts_t, g, b):
    n, d = base.shape
    nt, tm = pos_tiles.shape[0], pos_tiles.shape[2] // TOP_K
    pitch = _slab_pitch(d)
    tok = lambda i: (i, 0)
    const = lambda i: (0, 0)
    table = (None, 1, TOP_K * tm)
    return pl.pallas_call(
        _combine_body,
        grid=(nt,),
        in_specs=[pl.BlockSpec(table, lambda i: (i, 0, 0), memory_space=pltpu.SMEM),
                  pl.BlockSpec(table, lambda i: (jnp.minimum(i + 1, nt - 1), 0, 0),
                               memory_space=pltpu.SMEM),
                  pl.BlockSpec((tm, d), tok),
                  pl.BlockSpec((tm, TOP_K), tok),
                  pl.BlockSpec((1, d), const),
                  pl.BlockSpec((1, d), const),
                  pl.BlockSpec(memory_space=pl.ANY)],
        out_specs=[pl.BlockSpec((tm, d), tok), pl.BlockSpec((tm, d), tok)],
        out_shape=[jax.ShapeDtypeStruct((n, d), F32), jax.ShapeDtypeStruct((n, d), BF16)],
        scratch_shapes=[pltpu.VMEM((tm * pitch, LANES), U32) for _ in range(2 * TOP_K)]
        + [pltpu.VMEM((tm, d), F32), pltpu.SemaphoreType.DMA((2,))],
        compiler_params=_params(("arbitrary",), V7X_VMEM_LIMIT),
        name="combine_ln",
    )(pos_tiles, pos_tiles, base, wts_t, g.reshape(1, d), b.reshape(1, d), ys)


def moe_block(x, xb, w_router, router_bias, w1, w3, w2, layer, ws1, ws3, ws2, g, b, alpha,
              tm_tok=256, tm_exp=256, tm_comb=128):
    n, d = x.shape
    ne = w_router.shape[1]
    tm_tok, tm_exp = _fit(tm_tok, n), _fit(tm_exp, n)
    eidx, wts, rank, counts = router(x, w_router, router_bias)
    starts = jnp.cumsum(counts) - counts
    pos = rank + jnp.sum(jnp.where(eidx[..., None] == jnp.arange(ne), starts, 0), axis=-1)
    slab_row = pos * _slab_pitch(d)

    def tables(tm):
        return slab_row.reshape(TOP_K, n // tm, tm).transpose(1, 0, 2).reshape(n // tm, 1, TOP_K * tm)

    xs = dispatch(x, tables(tm_tok))
    items = _work_items(counts, TOP_K * n, tm_exp)
    ys = expert_ffn(xs, w1, w3, w2, layer, items, tm_exp)
    ws13 = jnp.concatenate([ws1.astype(BF16), ws3.astype(BF16)], axis=-1)
    base = shared_ffn_residual(x, xb, ws13, ws2.astype(BF16), alpha)
    return combine_ln(base, ys, tables(_fit(tm_comb, n)), wts.T, g, b)


def kernel(x, t5_table, w_in_even, diff_lambda, diff_gain, w_out_even, w_in_odd, na_rpb, w_out_odd,
           ln_mix_g, ln_mix_b, router_w, router_bias, exp_w1, exp_w3, exp_w2, sh_w1, sh_w3, sh_w2,
           ln_ffn_g, ln_ffn_b):
    bsz, s, d = x.shape
    depth = ln_mix_g.shape[0]
    alpha = (2 * depth) ** 0.25
    xf = x.reshape(bsz * s, d).astype(F32)
    xb = xf.astype(BF16)
    for l in range(depth):
        j = l // 2
        if l % 2 == 0:
            h = matmul(xb, w_in_even[j], BF16)
            lam_init = 0.8 - 0.6 * math.exp(-0.3 * l)
            att = even_attention(h, bsz, s, d, t5_table, diff_lambda[j], diff_gain[j], lam_init)
            w_out = w_out_even[j]
        else:
            h = matmul(xb, w_in_odd[j], BF16)
            att = odd_attention(h, bsz, s, d, na_rpb[j])
            w_out = w_out_odd[j]
        xf, xb = proj_res_ln(att, w_out.astype(BF16), xf, ln_mix_g[l], ln_mix_b[l], alpha)
        xf, xb = moe_block(xf, xb, router_w[l], router_bias[l], exp_w1, exp_w3, exp_w2, l,
                           sh_w1[l], sh_w3[l], sh_w2[l], ln_ffn_g[l], ln_ffn_b[l], alpha)
    return xf.reshape(bsz, s, d).astype(x.dtype)
```
